```python
import math
import jax, jax.numpy as jnp
from jax import lax
import numpy as np

D_MODEL = 2048
BATCH = 4
SEQ = 4096
DEPTH = 1

D_SSM = D_MODEL // 2
SSM_GROUP = 16
N_SSM_GROUPS = D_SSM // SSM_GROUP
SSM_STATE = 64
D_GMLP = D_MODEL - D_SSM
GMLP_HEAD = 128
N_GMLP_HEADS = D_GMLP // GMLP_HEAD
CHUNK = 128
D_IN = D_SSM + 2 * D_GMLP
D_FF = 5632
D_PLE = 256
EPS = 1e-6
DT_MIN = 1e-3
DT_MAX = 1e-1

kernel_name = "hybrid_s5_gmlp_macaron_ple"


def rms_norm(x, g):
    xf = x.astype(jnp.float32)
    y = xf * lax.rsqrt(jnp.mean(xf * xf, axis=-1, keepdims=True) + EPS) * g.astype(jnp.float32)
    return y.astype(x.dtype)


def layer_norm(x, g):
    xf = x.astype(jnp.float32)
    xc = xf - jnp.mean(xf, axis=-1, keepdims=True)
    y = xc * lax.rsqrt(jnp.mean(xc * xc, axis=-1, keepdims=True) + EPS) * g.astype(jnp.float32)
    return y.astype(x.dtype)


def swiglu(x, w_gate, w_up, w_down):
    return (jax.nn.silu(x @ w_gate) * (x @ w_up)) @ w_down


def _complex_affine_combine(e1, e2):
    a1r, a1i, b1r, b1i = e1
    a2r, a2i, b2r, b2i = e2
    return (a2r * a1r - a2i * a1i,
            a2r * a1i + a2i * a1r,
            a2r * b1r - a2i * b1i + b2r,
            a2r * b1i + a2i * b1r + b2i)


def s5_mixer(u, log_dt, a_re, a_im, b_re, b_im, c_re, c_im, d, w_glu):
    bsz, seqlen, _ = u.shape
    f32 = jnp.float32
    uf = u.astype(f32).reshape(bsz, seqlen, N_SSM_GROUPS, SSM_GROUP)
    dt = jnp.exp(log_dt.astype(f32))[:, None]
    lr = jnp.minimum(a_re.astype(f32), -1e-4)
    li = a_im.astype(f32)
    mag = jnp.exp(lr * dt)
    ang = li * dt
    abar_r = mag * jnp.cos(ang)
    abar_i = mag * jnp.sin(ang)
    den = lr * lr + li * li
    xr = abar_r - 1.0
    xi = abar_i
    zr = (xr * lr + xi * li) / den
    zi = (xi * lr - xr * li) / den
    br = b_re.astype(f32)
    bi = b_im.astype(f32)
    bbar_r = zr[..., None] * br - zi[..., None] * bi
    bbar_i = zr[..., None] * bi + zi[..., None] * br
    drive_r = jnp.einsum("blgp,gnp->lbgn", uf, bbar_r)
    drive_i = jnp.einsum("blgp,gnp->lbgn", uf, bbar_i)
    ar = jnp.broadcast_to(abar_r[None, None], (seqlen, 1, N_SSM_GROUPS, SSM_STATE))
    ai = jnp.broadcast_to(abar_i[None, None], (seqlen, 1, N_SSM_GROUPS, SSM_STATE))
    _, _, sr, si = lax.associative_scan(_complex_affine_combine, (ar, ai, drive_r, drive_i), axis=0)
    y = (jnp.einsum("lbgn,gpn->blgp", sr, c_re.astype(f32))
         - jnp.einsum("lbgn,gpn->blgp", si, c_im.astype(f32)))
    y = y + d.astype(f32).reshape(N_SSM_GROUPS, SSM_GROUP) * uf
    y = jax.nn.gelu(y.reshape(bsz, seqlen, D_SSM))
    y = y * jax.nn.sigmoid(y @ w_glu.astype(f32))
    return y.astype(u.dtype)


def gmlp_mixer(z_u, z_v, norm_v, w_s, b_s):
    bsz, seqlen, _ = z_u.shape
    n_chunks = seqlen // CHUNK
    u = jax.nn.gelu(z_u)
    v = layer_norm(jax.nn.gelu(z_v), norm_v)
    causal = jnp.tril(jnp.ones((CHUNK, CHUNK), dtype=bool))
    w = jnp.where(causal[None], w_s, jnp.zeros_like(w_s))
    vc = v.reshape(bsz, n_chunks, CHUNK, N_GMLP_HEADS, GMLP_HEAD)
    s = jnp.einsum("hts,bcshp->bcthp", w, vc) + b_s.T[None, None, :, :, None]
    out = u.reshape(bsz, n_chunks, CHUNK, N_GMLP_HEADS, GMLP_HEAD) * s
    return out.reshape(bsz, seqlen, D_GMLP)


def setup_inputs(seed: int = 0) -> dict:
    key = jax.random.key(seed)
    ks = jax.random.split(key, 32)
    f32 = jnp.float32

    def nrm(k, shape, std):
        return (jax.random.normal(k, shape, f32) * std).astype(f32)

    def gain(k, shape):
        return 1.0 + nrm(k, shape, 0.02)

    L = DEPTH
    x = nrm(ks[0], (BATCH, SEQ, D_MODEL), 1.0)
    p = nrm(ks[1], (L, BATCH, SEQ, D_PLE), 1.0)
    n_idx = jnp.arange(SSM_STATE, dtype=f32)
    return {
        "x": x,
        "p": p,
        "norm_ffn1": gain(ks[2], (L, D_MODEL)),
        "w1_gate": nrm(ks[3], (L, D_MODEL, D_FF), D_MODEL ** -0.5),
        "w1_up": nrm(ks[4], (L, D_MODEL, D_FF), D_MODEL ** -0.5),
        "w1_down": nrm(ks[5], (L, D_FF, D_MODEL), D_FF ** -0.5),
        "norm_mix": gain(ks[6], (L, D_MODEL)),
        "w_in": nrm(ks[7], (L, D_MODEL, D_IN), D_MODEL ** -0.5),
        "ssm_log_dt": jax.random.uniform(ks[8], (L, N_SSM_GROUPS), f32, math.log(DT_MIN), math.log(DT_MAX)),
        "ssm_a_re": -0.5 + nrm(ks[9], (L, N_SSM_GROUPS, SSM_STATE), 0.01),
        "ssm_a_im": math.pi * n_idx[None, None, :] + nrm(ks[10], (L, N_SSM_GROUPS, SSM_STATE), 0.01),
        "ssm_b_re": nrm(ks[11], (L, N_SSM_GROUPS, SSM_STATE, SSM_GROUP), (2 * SSM_GROUP) ** -0.5),
        "ssm_b_im": nrm(ks[12], (L, N_SSM_GROUPS, SSM_STATE, SSM_GROUP), (2 * SSM_GROUP) ** -0.5),
        "ssm_c_re": nrm(ks[13], (L, N_SSM_GROUPS, SSM_GROUP, SSM_STATE), 0.5 ** 0.5),
        "ssm_c_im": nrm(ks[14], (L, N_SSM_GROUPS, SSM_GROUP, SSM_STATE), 0.5 ** 0.5),
        "ssm_d": nrm(ks[15], (L, D_SSM), 1.0),
        "ssm_w_glu": nrm(ks[16], (L, D_SSM, D_SSM), D_SSM ** -0.5),
        "gmlp_norm_v": gain(ks[17], (L, D_GMLP)),
        "gmlp_w_s": nrm(ks[18], (L, N_GMLP_HEADS, CHUNK, CHUNK), CHUNK ** -0.5),
        "gmlp_b_s": 1.0 + nrm(ks[19], (L, N_GMLP_HEADS, CHUNK), 0.01),
        "norm_ssm_out": gain(ks[20], (L, D_SSM)),
        "norm_gmlp_out": gain(ks[21], (L, D_GMLP)),
        "w_out": nrm(ks[22], (L, D_MODEL, D_MODEL), D_MODEL ** -0.5),
        "norm_ffn2": gain(ks[23], (L, D_MODEL)),
        "w2_gate": nrm(ks[24], (L, D_MODEL, D_FF), D_MODEL ** -0.5),
        "w2_up": nrm(ks[25], (L, D_MODEL, D_FF), D_MODEL ** -0.5),
        "w2_down": nrm(ks[26], (L, D_FF, D_MODEL), D_FF ** -0.5),
        "norm_ple": gain(ks[27], (L, D_MODEL)),
        "w_ple_gate": nrm(ks[28], (L, D_MODEL, D_MODEL), D_MODEL ** -0.5),
        "w_ple_proj": nrm(ks[29], (L, D_PLE, D_MODEL), D_PLE ** -0.5),
        "norm_final": gain(ks[30], (D_MODEL,)),
    }


def reference(x, p, norm_ffn1, w1_gate, w1_up, w1_down, norm_mix, w_in,
              ssm_log_dt, ssm_a_re, ssm_a_im, ssm_b_re, ssm_b_im, ssm_c_re, ssm_c_im,
              ssm_d, ssm_w_glu, gmlp_norm_v, gmlp_w_s, gmlp_b_s,
              norm_ssm_out, norm_gmlp_out, w_out, norm_ffn2, w2_gate, w2_up, w2_down,
              norm_ple, w_ple_gate, w_ple_proj, norm_final):
    h = x
    for i in range(DEPTH):
        h = h + 0.5 * swiglu(rms_norm(h, norm_ffn1[i]), w1_gate[i], w1_up[i], w1_down[i])
        z = rms_norm(h, norm_mix[i]) @ w_in[i]
        z_ssm = z[..., :D_SSM]
        z_u = z[..., D_SSM:D_SSM + D_GMLP]
        z_v = z[..., D_SSM + D_GMLP:]
        y_ssm = s5_mixer(z_ssm, ssm_log_dt[i], ssm_a_re[i], ssm_a_im[i], ssm_b_re[i], ssm_b_im[i],
                         ssm_c_re[i], ssm_c_im[i], ssm_d[i], ssm_w_glu[i])
        y_gmlp = gmlp_mixer(z_u, z_v, gmlp_norm_v[i], gmlp_w_s[i], gmlp_b_s[i])
        y = jnp.concatenate([rms_norm(y_ssm, norm_ssm_out[i]), rms_norm(y_gmlp, norm_gmlp_out[i])], axis=-1)
        h = h + y @ w_out[i]
        h = h + 0.5 * swiglu(rms_norm(h, norm_ffn2[i]), w2_gate[i], w2_up[i], w2_down[i])
        gate = jax.nn.sigmoid(rms_norm(h, norm_ple[i]) @ w_ple_gate[i])
        h = h + gate * (p[i] @ w_ple_proj[i])
    return rms_norm(h, norm_final)
```

```python
import functools

import jax
import jax.numpy as jnp
from jax import lax
from jax.experimental import pallas as pl
from jax.experimental.pallas import tpu as pltpu

F32 = jnp.float32
BF16 = jnp.bfloat16
EPS = 1e-6

V7X_LANES = 128
V7X_SUBLANES = 8
V7X_VMEM_BYTES = 64 * 1024 * 1024

SSM_GROUP = 16
SSM_STATE = 64
SSM_BLOCK_GROUPS = 16
GMLP_CHUNK = 128


def _rms(x, g):
    ms = jnp.mean(x * x, axis=-1, keepdims=True)
    return x * lax.rsqrt(ms + EPS) * g


def _resident(shape):
    n = len(shape)
    return pl.BlockSpec(shape, lambda *_: (0,) * n, pipeline_mode=pl.Buffered(1))


def _ffn_kernel(x_ref, g_ref, wg_ref, wu_ref, wd_ref, o_ref, xn_ref):
    @pl.when(pl.program_id(1) == 0)
    def _():
        x = x_ref[...]
        xn_ref[...] = _rms(x, g_ref[...]).astype(BF16)
        o_ref[...] = x

    xn = xn_ref[...]
    gate = jnp.dot(xn, wg_ref[...], preferred_element_type=F32)
    up = jnp.dot(xn, wu_ref[...], preferred_element_type=F32)
    act = (gate * jax.nn.sigmoid(gate) * (0.5 * up)).astype(BF16)
    o_ref[...] += jnp.dot(act, wd_ref[...], preferred_element_type=F32)


def _ffn(x, g, wg, wu, wd, *, tm, tf):
    m, d = x.shape
    f = wg.shape[1]
    vmem = (2 * tm * d * 4) * 2 + tm * d * 2 + 3 * 2 * d * tf * 2 + 4 * tm * tf * 4
    return pl.pallas_call(
        _ffn_kernel,
        grid=(m // tm, f // tf),
        in_specs=[
            pl.BlockSpec((tm, d), lambda i, j: (i, 0)),
            pl.BlockSpec((1, d), lambda i, j: (0, 0)),
            pl.BlockSpec((d, tf), lambda i, j: (0, j)),
            pl.BlockSpec((d, tf), lambda i, j: (0, j)),
            pl.BlockSpec((tf, d), lambda i, j: (j, 0)),
        ],
        out_specs=pl.BlockSpec((tm, d), lambda i, j: (i, 0)),
        out_shape=jax.ShapeDtypeStruct((m, d), F32),
        scratch_shapes=[pltpu.VMEM((tm, d), BF16)],
        compiler_params=pltpu.CompilerParams(
            dimension_semantics=("parallel", "arbitrary"),
            vmem_limit_bytes=min(vmem + (8 << 20), V7X_VMEM_BYTES - (4 << 20)),
        ),
        name="ffn",
    )(x, g.reshape(1, d), wg, wu, wd)


def _zoh(logdt, a_re, a_im):
    dt = jnp.exp(logdt)
    lr = jnp.minimum(a_re, -1e-4)
    li = a_im
    mag = jnp.exp(lr * dt)
    ang = li * dt
    abr = mag * jnp.cos(ang)
    abi = mag * jnp.sin(ang)
    den = lr * lr + li * li
    xr = abr - 1.0
    xi = abi
    zr = (xr * lr + xi * li) / den
    zi = (xi * lr - xr * li) / den
    return abr, abi, zr, zi


def _s5_prep_kernel(ldt_ref, are_ref, aim_ref, ldtb_ref, areb_ref, aimb_ref, bre_ref, bim_ref, cim_ref,
                    abr_ref, abi_ref, bbr_ref, bbi_ref, cimn_ref):
    abr, abi, _, _ = _zoh(ldt_ref[...], are_ref[...], aim_ref[...])
    abr_ref[...] = abr
    abi_ref[...] = abi
    _, _, zr, zi = _zoh(ldtb_ref[...], areb_ref[...], aimb_ref[...])
    br = bre_ref[...]
    bi = bim_ref[...]
    bbr_ref[...] = zr * br - zi * bi
    bbi_ref[...] = zr * bi + zi * br
    cimn_ref[...] = -cim_ref[...]


def _s5_prep(log_dt, a_re, a_im, b_re, b_im, c_im):
    g, n = a_re.shape
    p = b_re.shape[2]
    small = (g * n // V7X_LANES, V7X_LANES)
    big = (g * p * n // V7X_LANES, V7X_LANES)

    def bc(a):
        return jnp.broadcast_to(a[:, None, :], (g, p, n)).reshape(big)

    ldt = jnp.broadcast_to(log_dt[:, None], (g, n))
    outs = pl.pallas_call(
        _s5_prep_kernel,
        out_shape=[jax.ShapeDtypeStruct(small, F32)] * 2 + [jax.ShapeDtypeStruct(big, F32)] * 3,
        name="s5_prep",
    )(ldt.reshape(small), a_re.reshape(small), a_im.reshape(small),
      bc(ldt), bc(a_re), bc(a_im),
      jnp.swapaxes(b_re, 1, 2).reshape(big), jnp.swapaxes(b_im, 1, 2).reshape(big), c_im.reshape(big))
    abr, abi, bbr, bbi, cimn = outs
    return (abr.reshape(g, n), abi.reshape(g, n),
            bbr.reshape(g, p, n), bbi.reshape(g, p, n), cimn.reshape(g, p, n))


def _block_diag_in(w):
    g, p, n = w.shape
    nb = g // SSM_BLOCK_GROUPS
    w4 = w.reshape(nb, SSM_BLOCK_GROUPS, p, n)
    eye = jnp.eye(SSM_BLOCK_GROUPS, dtype=w.dtype)
    return jnp.einsum("kipn,ij->kipjn", w4, eye).reshape(nb, SSM_BLOCK_GROUPS * p, SSM_BLOCK_GROUPS * n)


def _block_diag_out(w):
    g, p, n = w.shape
    nb = g // SSM_BLOCK_GROUPS
    w4 = w.reshape(nb, SSM_BLOCK_GROUPS, p, n)
    eye = jnp.eye(SSM_BLOCK_GROUPS, dtype=w.dtype)
    return jnp.einsum("kipn,ij->kinjp", w4, eye).reshape(nb, SSM_BLOCK_GROUPS * n, SSM_BLOCK_GROUPS * p)


def _s5_kernel(h_ref, gmix_ref, win_ref, bblk_ref, cblk_ref, abr_ref, abi_ref, d_ref, wglu_ref, gout_ref,
               o_ref, dre_ref, dim_ref, st_ref, *, tc, nblk, bch, bst):
    ntile = bst // V7X_LANES
    nb = tc // V7X_SUBLANES

    @pl.when(pl.program_id(1) == 0)
    def _():
        st_ref[...] = jnp.zeros_like(st_ref)

    hn = _rms(h_ref[0], gmix_ref[...]).astype(BF16)
    u = jnp.dot(hn, win_ref[...], preferred_element_type=F32)
    ub = u.astype(BF16)

    for k in range(nblk):
        drv = jnp.dot(ub[:, k * bch:(k + 1) * bch], bblk_ref[k], preferred_element_type=F32)
        for j in range(ntile):
            lo = j * V7X_LANES
            rows = slice(j * V7X_SUBLANES, (j + 1) * V7X_SUBLANES)
            dre_ref[k, :, rows, :] = drv[:, lo:lo + V7X_LANES].reshape(nb, V7X_SUBLANES, V7X_LANES)
            dim_ref[k, :, rows, :] = drv[:, bst + lo:bst + lo + V7X_LANES].reshape(nb, V7X_SUBLANES, V7X_LANES)

    ar = [abr_ref[k] for k in range(nblk)]
    ai = [abi_ref[k] for k in range(nblk)]

    def body(tb, carry):
        s = list(carry)
        for r in range(V7X_SUBLANES):
            step = pl.ds(r, ntile, stride=V7X_SUBLANES)
            for k in range(nblk):
                sre, sim = s[2 * k], s[2 * k + 1]
                nre = ar[k] * sre - ai[k] * sim + dre_ref[k, tb, step, :]
                nim = ar[k] * sim + ai[k] * sre + dim_ref[k, tb, step, :]
                dre_ref[k, tb, step, :] = nre
                dim_ref[k, tb, step, :] = nim
                s[2 * k], s[2 * k + 1] = nre, nim
        return tuple(s)

    fin = lax.fori_loop(0, nb, body, tuple(st_ref[i] for i in range(2 * nblk)))
    for i in range(2 * nblk):
        st_ref[i] = fin[i]

    ys = []
    for k in range(nblk):
        parts = []
        for ref in (dre_ref, dim_ref):
            for j in range(ntile):
                rows = slice(j * V7X_SUBLANES, (j + 1) * V7X_SUBLANES)
                parts.append(ref[k, :, rows, :].reshape(tc, V7X_LANES))
        lhs = jnp.concatenate(parts, axis=1).astype(BF16)
        ys.append(jnp.dot(lhs, cblk_ref[k], preferred_element_type=F32))
    y = jnp.concatenate(ys, axis=1) + d_ref[...] * u
    y = jax.nn.gelu(y)
    y = y * jax.nn.sigmoid(jnp.dot(y.astype(BF16), wglu_ref[...], preferred_element_type=F32))
    o_ref[0] = _rms(y, gout_ref[...]).astype(BF16)


def _s5(h3, gmix, win, bblk, cblk, abr, abi, dskip, wglu, gout, *, tc):
    b, l, d = h3.shape
    dssm = win.shape[1]
    nblk, bch, bst2 = bblk.shape
    bst = bst2 // 2
    ntile = bst // V7X_LANES
    assert ntile == V7X_SUBLANES and tc % V7X_SUBLANES == 0
    nb = tc // V7X_SUBLANES
    kern = functools.partial(_s5_kernel, tc=tc, nblk=nblk, bch=bch, bst=bst)
    return pl.pallas_call(
        kern,
        grid=(b, l // tc),
        in_specs=[
            pl.BlockSpec((1, tc, d), lambda i, j: (i, j, 0)),
            _resident((1, d)),
            _resident((d, dssm)),
            _resident(bblk.shape),
            _resident(cblk.shape),
            _resident(abr.shape),
            _resident(abi.shape),
            _resident((1, dssm)),
            _resident(wglu.shape),
            _resident((1, dssm)),
        ],
        out_specs=pl.BlockSpec((1, tc, dssm), lambda i, j: (i, j, 0)),
        out_shape=jax.ShapeDtypeStruct((b, l, dssm), BF16),
        scratch_shapes=[
            pltpu.VMEM((nblk, nb, ntile * V7X_SUBLANES, V7X_LANES), F32),
            pltpu.VMEM((nblk, nb, ntile * V7X_SUBLANES, V7X_LANES), F32),
            pltpu.VMEM((2 * nblk, V7X_SUBLANES, V7X_LANES), F32),
        ],
        compiler_params=pltpu.CompilerParams(
            dimension_semantics=("parallel", "arbitrary"),
            vmem_limit_bytes=48 << 20,
        ),
        name="s5",
    )(h3, gmix.reshape(1, d), win, bblk, cblk, abr, abi, dskip.reshape(1, dssm), wglu, gout.reshape(1, dssm))


def _gmlp_kernel(h_ref, gmix_ref, wuv_ref, gv_ref, ws_ref, bs_ref, gout_ref, o_ref, *, tm, dg):
    hn = _rms(h_ref[...], gmix_ref[...]).astype(BF16)
    zu = jnp.dot(hn, wuv_ref[:, :dg], preferred_element_type=F32)
    zv = jnp.dot(hn, wuv_ref[:, dg:], preferred_element_type=F32)
    u = jax.nn.gelu(zu)
    v = jax.nn.gelu(zv)
    vc = v - jnp.mean(v, axis=-1, keepdims=True)
    vn = vc * lax.rsqrt(jnp.mean(vc * vc, axis=-1, keepdims=True) + EPS) * gv_ref[...]
    vb = vn.astype(BF16)

    nh = dg // GMLP_CHUNK
    t_idx = lax.broadcasted_iota(jnp.int32, (GMLP_CHUNK, GMLP_CHUNK), 0)
    s_idx = lax.broadcasted_iota(jnp.int32, (GMLP_CHUNK, GMLP_CHUNK), 1)
    causal = t_idx >= s_idx
    wm = [jnp.where(causal, ws_ref[hd], 0.0).astype(BF16) for hd in range(nh)]
    rows = []
    for c in range(tm // GMLP_CHUNK):
        r0 = c * GMLP_CHUNK
        cols = []
        for hd in range(nh):
            c0 = hd * GMLP_CHUNK
            s = jnp.dot(wm[hd], vb[r0:r0 + GMLP_CHUNK, c0:c0 + GMLP_CHUNK], preferred_element_type=F32)
            cols.append(s + bs_ref[hd])
        rows.append(jnp.concatenate(cols, axis=1))
    yg = u * jnp.concatenate(rows, axis=0)
    o_ref[...] = _rms(yg, gout_ref[...]).astype(BF16)


def _gmlp(h, gmix, wuv, gv, ws, bs_full, gout, *, tm):
    m, d = h.shape
    dg = wuv.shape[1] // 2
    kern = functools.partial(_gmlp_kernel, tm=tm, dg=dg)
    return pl.pallas_call(
        kern,
        grid=(m // tm,),
        in_specs=[
            pl.BlockSpec((tm, d), lambda i: (i, 0)),
            _resident((1, d)),
            _resident(wuv.shape),
            _resident((1, dg)),
            _resident(ws.shape),
            _resident(bs_full.shape),
            _resident((1, dg)),
        ],
        out_specs=pl.BlockSpec((tm, dg), lambda i: (i, 0)),
        out_shape=jax.ShapeDtypeStruct((m, dg), BF16),
        compiler_params=pltpu.CompilerParams(
            dimension_semantics=("parallel",),
            vmem_limit_bytes=48 << 20,
        ),
        name="gmlp",
    )(h, gmix.reshape(1, d), wuv, gv.reshape(1, dg), ws, bs_full, gout.reshape(1, dg))


def _outproj_kernel(h_ref, ys_ref, yg_ref, wo_ref, o_ref, *, ds):
    acc = jnp.dot(ys_ref[...], wo_ref[:ds, :], preferred_element_type=F32)
    acc += jnp.dot(yg_ref[...], wo_ref[ds:, :], preferred_element_type=F32)
    o_ref[...] = h_ref[...] + acc


def _outproj(h, ys, yg, wo, *, tm):
    m, d = h.shape
    ds = ys.shape[1]
    dg = yg.shape[1]
    kern = functools.partial(_outproj_kernel, ds=ds)
    return pl.pallas_call(
        kern,
        grid=(m // tm,),
        in_specs=[
            pl.BlockSpec((tm, d), lambda i: (i, 0)),
            pl.BlockSpec((tm, ds), lambda i: (i, 0)),
            pl.BlockSpec((tm, dg), lambda i: (i, 0)),
            _resident(wo.shape),
        ],
        out_specs=pl.BlockSpec((tm, d), lambda i: (i, 0)),
        out_shape=jax.ShapeDtypeStruct((m, d), F32),
        compiler_params=pltpu.CompilerParams(
            dimension_semantics=("parallel",),
            vmem_limit_bytes=48 << 20,
        ),
        name="outproj",
    )(h, ys, yg, wo)


def _ple_kernel(h_ref, p_ref, gple_ref, wg_ref, wp_ref, gfin_ref, o_ref):
    h = h_ref[...]
    hn = _rms(h, gple_ref[...]).astype(BF16)
    gate = jax.nn.sigmoid(jnp.dot(hn, wg_ref[...], preferred_element_type=F32))
    pp = jnp.dot(p_ref[...].astype(BF16), wp_ref[...], preferred_element_type=F32)
    o_ref[...] = _rms(h + gate * pp, gfin_ref[...])


def _ple(h, p, gple, wg, wp, gfin, *, tm):
    m, d = h.shape
    dp = p.shape[1]
    return pl.pallas_call(
        _ple_kernel,
        grid=(m // tm,),
        in_specs=[
            pl.BlockSpec((tm, d), lambda i: (i, 0)),
            pl.BlockSpec((tm, dp), lambda i: (i, 0)),
            _resident((1, d)),
            _resident(wg.shape),
            _resident(wp.shape),
            _resident((1, d)),
        ],
        out_specs=pl.BlockSpec((tm, d), lambda i: (i, 0)),
        out_shape=jax.ShapeDtypeStruct((m, d), F32),
        compiler_params=pltpu.CompilerParams(
            dimension_semantics=("parallel",),
            vmem_limit_bytes=48 << 20,
        ),
        name="ple",
    )(h, p, gple.reshape(1, d), wg, wp, gfin.reshape(1, d))


def kernel(x, p, norm_ffn1, w1_gate, w1_up, w1_down, norm_mix, w_in, ssm_log_dt, ssm_a_re, ssm_a_im, ssm_b_re, ssm_b_im, ssm_c_re, ssm_c_im, ssm_d, ssm_w_glu, gmlp_norm_v, gmlp_w_s, gmlp_b_s, norm_ssm_out, norm_gmlp_out, w_out, norm_ffn2, w2_gate, w2_up, w2_down, norm_ple, w_ple_gate, w_ple_proj, norm_final):
    bsz, seqlen, d = x.shape
    depth = p.shape[0]
    m = bsz * seqlen
    dssm = ssm_d.shape[1]
    h = x.reshape(m, d)
    for i in range(depth):
        h = _ffn(h, norm_ffn1[i], w1_gate[i].astype(BF16), w1_up[i].astype(BF16), w1_down[i].astype(BF16),
                 tm=512, tf=512)

        abr, abi, bbr, bbi, cimn = _s5_prep(ssm_log_dt[i], ssm_a_re[i], ssm_a_im[i],
                                            ssm_b_re[i], ssm_b_im[i], ssm_c_im[i])
        bblk = jnp.concatenate([_block_diag_in(bbr), _block_diag_in(bbi)], axis=2).astype(BF16)
        cblk = jnp.concatenate([_block_diag_out(ssm_c_re[i]), _block_diag_out(cimn)], axis=1).astype(BF16)
        nblk = bblk.shape[0]
        ab_shape = (nblk, V7X_SUBLANES, V7X_LANES)
        w_in_b = w_in[i].astype(BF16)
        ys = _s5(h.reshape(bsz, seqlen, d), norm_mix[i], w_in_b[:, :dssm], bblk, cblk,
                 abr.reshape(ab_shape), abi.reshape(ab_shape), ssm_d[i], ssm_w_glu[i].astype(BF16),
                 norm_ssm_out[i], tc=256)

        nh, ck = gmlp_b_s.shape[1:]
        bs_full = jnp.broadcast_to(gmlp_b_s[i][:, :, None], (nh, ck, ck))
        yg = _gmlp(h, norm_mix[i], w_in_b[:, dssm:], gmlp_norm_v[i], gmlp_w_s[i], bs_full,
                   norm_gmlp_out[i], tm=512)

        h = _outproj(h, ys.reshape(m, dssm), yg, w_out[i].astype(BF16), tm=512)

        h = _ffn(h, norm_ffn2[i], w2_gate[i].astype(BF16), w2_up[i].astype(BF16), w2_down[i].astype(BF16),
                 tm=512, tf=512)

        norm_out = norm_final if i == depth - 1 else None
        assert norm_out is not None, "per-layer embedding kernel fuses the final norm; depth must be 1"
        h = _ple(h, p[i].reshape(m, -1), norm_ple[i], w_ple_gate[i].astype(BF16), w_ple_proj[i].astype(BF16),
                 norm_out, tm=512)
    return h.reshape(bsz, seqlen, d)
```

```python
import functools

import jax
import jax.numpy as jnp
from jax import lax
from jax.experimental import pallas as pl
from jax.experimental.pallas import tpu as pltpu

F32 = jnp.float32
BF16 = jnp.bfloat16
EPS = 1e-6

V7X_LANES = 128
V7X_SUBLANES = 8
V7X_VMEM_BYTES = 64 * 1024 * 1024

SSM_GROUP = 16
SSM_STATE = 64
SSM_BLOCK_GROUPS = 16
GMLP_CHUNK = 128


def _rms(x, g):
    ms = jnp.mean(x * x, axis=-1, keepdims=True)
    return x * lax.rsqrt(ms + EPS) * g


def _resident(shape):
    n = len(shape)
    return pl.BlockSpec(shape, lambda *_: (0,) * n, pipeline_mode=pl.Buffered(1))


def _ffn_kernel(x_ref, g_ref, wg_ref, wu_ref, wd_ref, o_ref, xn_ref):
    @pl.when(pl.program_id(1) == 0)
    def _():
        x = x_ref[...]
        xn_ref[...] = _rms(x, g_ref[...]).astype(BF16)
        o_ref[...] = x

    xn = xn_ref[...]
    gate = jnp.dot(xn, wg_ref[...], preferred_element_type=F32)
    up = jnp.dot(xn, wu_ref[...], preferred_element_type=F32)
    act = (gate * jax.nn.sigmoid(gate) * (0.5 * up)).astype(BF16)
    o_ref[...] += jnp.dot(act, wd_ref[...], preferred_element_type=F32)


def _ffn(x, g, wg, wu, wd, *, tm, tf):
    m, d = x.shape
    f = wg.shape[1]
    vmem = (2 * tm * d * 4) * 2 + tm * d * 2 + 3 * 2 * d * tf * 2 + 4 * tm * tf * 4
    return pl.pallas_call(
        _ffn_kernel,
        grid=(m // tm, f // tf),
        in_specs=[
            pl.BlockSpec((tm, d), lambda i, j: (i, 0)),
            pl.BlockSpec((1, d), lambda i, j: (0, 0)),
            pl.BlockSpec((d, tf), lambda i, j: (0, j)),
            pl.BlockSpec((d, tf), lambda i, j: (0, j)),
            pl.BlockSpec((tf, d), lambda i, j: (j, 0)),
        ],
        out_specs=pl.BlockSpec((tm, d), lambda i, j: (i, 0)),
        out_shape=jax.ShapeDtypeStruct((m, d), F32),
        scratch_shapes=[pltpu.VMEM((tm, d), BF16)],
        compiler_params=pltpu.CompilerParams(
            dimension_semantics=("parallel", "arbitrary"),
            vmem_limit_bytes=min(vmem + (8 << 20), V7X_VMEM_BYTES - (2 << 20)),
        ),
        name="ffn",
    )(x, g.reshape(1, d), wg, wu, wd)


def _zoh(logdt, a_re, a_im):
    dt = jnp.exp(logdt)
    lr = jnp.minimum(a_re, -1e-4)
    li = a_im
    mag = jnp.exp(lr * dt)
    ang = li * dt
    abr = mag * jnp.cos(ang)
    abi = mag * jnp.sin(ang)
    den = lr * lr + li * li
    xr = abr - 1.0
    xi = abi
    zr = (xr * lr + xi * li) / den
    zi = (xi * lr - xr * li) / den
    return abr, abi, zr, zi


def _s5_prep_kernel(ldt_ref, are_ref, aim_ref, ldtb_ref, areb_ref, aimb_ref, bre_ref, bim_ref, cim_ref,
                    abr_ref, abi_ref, bbr_ref, bbi_ref, cimn_ref):
    abr, abi, _, _ = _zoh(ldt_ref[...], are_ref[...], aim_ref[...])
    abr_ref[...] = abr
    abi_ref[...] = abi
    _, _, zr, zi = _zoh(ldtb_ref[...], areb_ref[...], aimb_ref[...])
    br = bre_ref[...]
    bi = bim_ref[...]
    bbr_ref[...] = zr * br - zi * bi
    bbi_ref[...] = zr * bi + zi * br
    cimn_ref[...] = -cim_ref[...]


def _s5_prep(log_dt, a_re, a_im, b_re, b_im, c_im):
    g, n = a_re.shape
    p = b_re.shape[2]
    small = (g * n // V7X_LANES, V7X_LANES)
    big = (g * p * n // V7X_LANES, V7X_LANES)

    def bc(a):
        return jnp.broadcast_to(a[:, None, :], (g, p, n)).reshape(big)

    ldt = jnp.broadcast_to(log_dt[:, None], (g, n))
    outs = pl.pallas_call(
        _s5_prep_kernel,
        out_shape=[jax.ShapeDtypeStruct(small, F32)] * 2 + [jax.ShapeDtypeStruct(big, F32)] * 3,
        name="s5_prep",
    )(ldt.reshape(small), a_re.reshape(small), a_im.reshape(small),
      bc(ldt), bc(a_re), bc(a_im),
      jnp.swapaxes(b_re, 1, 2).reshape(big), jnp.swapaxes(b_im, 1, 2).reshape(big), c_im.reshape(big))
    abr, abi, bbr, bbi, cimn = outs
    return (abr.reshape(g, n), abi.reshape(g, n),
            bbr.reshape(g, p, n), bbi.reshape(g, p, n), cimn.reshape(g, p, n))


def _block_diag_in(w):
    g, p, n = w.shape
    nb = g // SSM_BLOCK_GROUPS
    w4 = w.reshape(nb, SSM_BLOCK_GROUPS, p, n)
    eye = jnp.eye(SSM_BLOCK_GROUPS, dtype=w.dtype)
    return jnp.einsum("kipn,ij->kipjn", w4, eye).reshape(nb, SSM_BLOCK_GROUPS * p, SSM_BLOCK_GROUPS * n)


def _block_diag_out(w):
    g, p, n = w.shape
    nb = g // SSM_BLOCK_GROUPS
    w4 = w.reshape(nb, SSM_BLOCK_GROUPS, p, n)
    eye = jnp.eye(SSM_BLOCK_GROUPS, dtype=w.dtype)
    return jnp.einsum("kipn,ij->kinjp", w4, eye).reshape(nb, SSM_BLOCK_GROUPS * n, SSM_BLOCK_GROUPS * p)


def _s5_kernel(h_ref, gmix_ref, win_ref, bblk_ref, cblk_ref, abr_ref, abi_ref, d_ref, wglu_ref, gout_ref,
               o_ref, dre_ref, dim_ref, st_ref, *, tc, nblk, bch, bst):
    ntile = bst // V7X_LANES
    nb = tc // V7X_SUBLANES

    @pl.when(pl.program_id(1) == 0)
    def _():
        st_ref[...] = jnp.zeros_like(st_ref)

    hn = _rms(h_ref[0], gmix_ref[...]).astype(BF16)
    u = jnp.dot(hn, win_ref[...], preferred_element_type=F32)
    ub = u.astype(BF16)

    for k in range(nblk):
        drv = jnp.dot(ub[:, k * bch:(k + 1) * bch], bblk_ref[k], preferred_element_type=F32)
        for j in range(ntile):
            lo = j * V7X_LANES
            rows = slice(j * V7X_SUBLANES, (j + 1) * V7X_SUBLANES)
            dre_ref[k, :, rows, :] = drv[:, lo:lo + V7X_LANES].reshape(nb, V7X_SUBLANES, V7X_LANES)
            dim_ref[k, :, rows, :] = drv[:, bst + lo:bst + lo + V7X_LANES].reshape(nb, V7X_SUBLANES, V7X_LANES)

    ar = [abr_ref[k] for k in range(nblk)]
    ai = [abi_ref[k] for k in range(nblk)]

    def body(tb, carry):
        s = list(carry)
        for r in range(V7X_SUBLANES):
            step = pl.ds(r, ntile, stride=V7X_SUBLANES)
            for k in range(nblk):
                sre, sim = s[2 * k], s[2 * k + 1]
                nre = ar[k] * sre - ai[k] * sim + dre_ref[k, tb, step, :]
                nim = ar[k] * sim + ai[k] * sre + dim_ref[k, tb, step, :]
                dre_ref[k, tb, step, :] = nre
                dim_ref[k, tb, step, :] = nim
                s[2 * k], s[2 * k + 1] = nre, nim
        return tuple(s)

    fin = lax.fori_loop(0, nb, body, tuple(st_ref[i] for i in range(2 * nblk)))
    for i in range(2 * nblk):
        st_ref[i] = fin[i]

    ys = []
    for k in range(nblk):
        parts = []
        for ref in (dre_ref, dim_ref):
            for j in range(ntile):
                rows = slice(j * V7X_SUBLANES, (j + 1) * V7X_SUBLANES)
                parts.append(ref[k, :, rows, :].reshape(tc, V7X_LANES))
        lhs = jnp.concatenate(parts, axis=1).astype(BF16)
        ys.append(jnp.dot(lhs, cblk_ref[k], preferred_element_type=F32))
    y = jnp.concatenate(ys, axis=1) + d_ref[...] * u
    y = jax.nn.gelu(y)
    y = y * jax.nn.sigmoid(jnp.dot(y.astype(BF16), wglu_ref[...], preferred_element_type=F32))
    o_ref[0] = _rms(y, gout_ref[...]).astype(BF16)


def _s5(h3, gmix, win, bblk, cblk, abr, abi, dskip, wglu, gout, *, tc):
    b, l, d = h3.shape
    dssm = win.shape[1]
    nblk, bch, bst2 = bblk.shape
    bst = bst2 // 2
    ntile = bst // V7X_LANES
    assert ntile == V7X_SUBLANES and tc % V7X_SUBLANES == 0
    nb = tc // V7X_SUBLANES
    kern = functools.partial(_s5_kernel, tc=tc, nblk=nblk, bch=bch, bst=bst)
    return pl.pallas_call(
        kern,
        grid=(b, l // tc),
        in_specs=[
            pl.BlockSpec((1, tc, d), lambda i, j: (i, j, 0)),
            _resident((1, d)),
            _resident((d, dssm)),
            _resident(bblk.shape),
            _resident(cblk.shape),
            _resident(abr.shape),
            _resident(abi.shape),
            _resident((1, dssm)),
            _resident(wglu.shape),
            _resident((1, dssm)),
        ],
        out_specs=pl.BlockSpec((1, tc, dssm), lambda i, j: (i, j, 0)),
        out_shape=jax.ShapeDtypeStruct((b, l, dssm), BF16),
        scratch_shapes=[
            pltpu.VMEM((nblk, nb, ntile * V7X_SUBLANES, V7X_LANES), F32),
            pltpu.VMEM((nblk, nb, ntile * V7X_SUBLANES, V7X_LANES), F32),
            pltpu.VMEM((2 * nblk, V7X_SUBLANES, V7X_LANES), F32),
        ],
        compiler_params=pltpu.CompilerParams(
            dimension_semantics=("parallel", "arbitrary"),
            vmem_limit_bytes=48 << 20,
        ),
        name="s5",
    )(h3, gmix.reshape(1, d), win, bblk, cblk, abr, abi, dskip.reshape(1, dssm), wglu, gout.reshape(1, dssm))


def _gmlp_kernel(h_ref, gmix_ref, wuv_ref, gv_ref, ws_ref, bs_ref, gout_ref, o_ref, *, tm, dg):
    hn = _rms(h_ref[...], gmix_ref[...]).astype(BF16)
    zu = jnp.dot(hn, wuv_ref[:, :dg], preferred_element_type=F32)
    zv = jnp.dot(hn, wuv_ref[:, dg:], preferred_element_type=F32)
    u = jax.nn.gelu(zu)
    v = jax.nn.gelu(zv)
    vc = v - jnp.mean(v, axis=-1, keepdims=True)
    vn = vc * lax.rsqrt(jnp.mean(vc * vc, axis=-1, keepdims=True) + EPS) * gv_ref[...]
    vb = vn.astype(BF16)

    nh = dg // GMLP_CHUNK
    t_idx = lax.broadcasted_iota(jnp.int32, (GMLP_CHUNK, GMLP_CHUNK), 0)
    s_idx = lax.broadcasted_iota(jnp.int32, (GMLP_CHUNK, GMLP_CHUNK), 1)
    causal = t_idx >= s_idx
    wm = [jnp.where(causal, ws_ref[hd], 0.0).astype(BF16) for hd in range(nh)]
    rows = []
    for c in range(tm // GMLP_CHUNK):
        r0 = c * GMLP_CHUNK
        cols = []
        for hd in range(nh):
            c0 = hd * GMLP_CHUNK
            s = jnp.dot(wm[hd], vb[r0:r0 + GMLP_CHUNK, c0:c0 + GMLP_CHUNK], preferred_element_type=F32)
            cols.append(s + bs_ref[hd])
        rows.append(jnp.concatenate(cols, axis=1))
    yg = u * jnp.concatenate(rows, axis=0)
    o_ref[...] = _rms(yg, gout_ref[...]).astype(BF16)


def _gmlp(h, gmix, wuv, gv, ws, bs_full, gout, *, tm):
    m, d = h.shape
    dg = wuv.shape[1] // 2
    kern = functools.partial(_gmlp_kernel, tm=tm, dg=dg)
    return pl.pallas_call(
        kern,
        grid=(m // tm,),
        in_specs=[
            pl.BlockSpec((tm, d), lambda i: (i, 0)),
            _resident((1, d)),
            _resident(wuv.shape),
            _resident((1, dg)),
            _resident(ws.shape),
            _resident(bs_full.shape),
            _resident((1, dg)),
        ],
        out_specs=pl.BlockSpec((tm, dg), lambda i: (i, 0)),
        out_shape=jax.ShapeDtypeStruct((m, dg), BF16),
        compiler_params=pltpu.CompilerParams(
            dimension_semantics=("parallel",),
            vmem_limit_bytes=48 << 20,
        ),
        name="gmlp",
    )(h, gmix.reshape(1, d), wuv, gv.reshape(1, dg), ws, bs_full, gout.reshape(1, dg))


def _outproj_kernel(h_ref, ys_ref, yg_ref, wo_ref, o_ref, *, ds):
    acc = jnp.dot(ys_ref[...], wo_ref[:ds, :], preferred_element_type=F32)
    acc += jnp.dot(yg_ref[...], wo_ref[ds:, :], preferred_element_type=F32)
    o_ref[...] = h_ref[...] + acc


def _outproj(h, ys, yg, wo, *, tm):
    m, d = h.shape
    ds = ys.shape[1]
    dg = yg.shape[1]
    kern = functools.partial(_outproj_kernel, ds=ds)
    return pl.pallas_call(
        kern,
        grid=(m // tm,),
        in_specs=[
            pl.BlockSpec((tm, d), lambda i: (i, 0)),
            pl.BlockSpec((tm, ds), lambda i: (i, 0)),
            pl.BlockSpec((tm, dg), lambda i: (i, 0)),
            _resident(wo.shape),
        ],
        out_specs=pl.BlockSpec((tm, d), lambda i: (i, 0)),
        out_shape=jax.ShapeDtypeStruct((m, d), F32),
        compiler_params=pltpu.CompilerParams(
            dimension_semantics=("parallel",),
            vmem_limit_bytes=48 << 20,
        ),
        name="outproj",
    )(h, ys, yg, wo)


def _ple_kernel(h_ref, p_ref, gple_ref, wg_ref, wp_ref, gfin_ref, o_ref):
    h = h_ref[...]
    hn = _rms(h, gple_ref[...]).astype(BF16)
    gate = jax.nn.sigmoid(jnp.dot(hn, wg_ref[...], preferred_element_type=F32))
    pp = jnp.dot(p_ref[...].astype(BF16), wp_ref[...], preferred_element_type=F32)
    o_ref[...] = _rms(h + gate * pp, gfin_ref[...])


def _ple(h, p, gple, wg, wp, gfin, *, tm):
    m, d = h.shape
    dp = p.shape[1]
    return pl.pallas_call(
        _ple_kernel,
        grid=(m // tm,),
        in_specs=[
            pl.BlockSpec((tm, d), lambda i: (i, 0)),
            pl.BlockSpec((tm, dp), lambda i: (i, 0)),
            _resident((1, d)),
            _resident(wg.shape),
            _resident(wp.shape),
            _resident((1, d)),
        ],
        out_specs=pl.BlockSpec((tm, d), lambda i: (i, 0)),
        out_shape=jax.ShapeDtypeStruct((m, d), F32),
        compiler_params=pltpu.CompilerParams(
            dimension_semantics=("parallel",),
            vmem_limit_bytes=48 << 20,
        ),
        name="ple",
    )(h, p, gple.reshape(1, d), wg, wp, gfin.reshape(1, d))


def kernel(x, p, norm_ffn1, w1_gate, w1_up, w1_down, norm_mix, w_in, ssm_log_dt, ssm_a_re, ssm_a_im, ssm_b_re, ssm_b_im, ssm_c_re, ssm_c_im, ssm_d, ssm_w_glu, gmlp_norm_v, gmlp_w_s, gmlp_b_s, norm_ssm_out, norm_gmlp_out, w_out, norm_ffn2, w2_gate, w2_up, w2_down, norm_ple, w_ple_gate, w_ple_proj, norm_final):
    bsz, seqlen, d = x.shape
    depth = p.shape[0]
    m = bsz * seqlen
    dssm = ssm_d.shape[1]
    h = x.reshape(m, d)
    for i in range(depth):
        h = _ffn(h, norm_ffn1[i], w1_gate[i].astype(BF16), w1_up[i].astype(BF16), w1_down[i].astype(BF16),
                 tm=1024, tf=512)

        abr, abi, bbr, bbi, cimn = _s5_prep(ssm_log_dt[i], ssm_a_re[i], ssm_a_im[i],
                                            ssm_b_re[i], ssm_b_im[i], ssm_c_im[i])
        bblk = jnp.concatenate([_block_diag_in(bbr), _block_diag_in(bbi)], axis=2).astype(BF16)
        cblk = jnp.concatenate([_block_diag_out(ssm_c_re[i]), _block_diag_out(cimn)], axis=1).astype(BF16)
        nblk = bblk.shape[0]
        ab_shape = (nblk, V7X_SUBLANES, V7X_LANES)
        w_in_b = w_in[i].astype(BF16)
        ys = _s5(h.reshape(bsz, seqlen, d), norm_mix[i], w_in_b[:, :dssm], bblk, cblk,
                 abr.reshape(ab_shape), abi.reshape(ab_shape), ssm_d[i], ssm_w_glu[i].astype(BF16),
                 norm_ssm_out[i], tc=256)

        nh, ck = gmlp_b_s.shape[1:]
        bs_full = jnp.broadcast_to(gmlp_b_s[i][:, :, None], (nh, ck, ck))
        yg = _gmlp(h, norm_mix[i], w_in_b[:, dssm:], gmlp_norm_v[i], gmlp_w_s[i], bs_full,
                   norm_gmlp_out[i], tm=512)

        h = _outproj(h, ys.reshape(m, dssm), yg, w_out[i].astype(BF16), tm=512)

        h = _ffn(h, norm_ffn2[i], w2_gate[i].astype(BF16), w2_up[i].astype(BF16), w2_down[i].astype(BF16),
                 tm=1024, tf=512)

        norm_out = norm_final if i == depth - 1 else None
        assert norm_out is not None, "per-layer embedding kernel fuses the final norm; depth must be 1"
        h = _ple(h, p[i].reshape(m, -1), norm_ple[i], w_ple_gate[i].astype(BF16), w_ple_proj[i].astype(BF16),
                 norm_out, tm=512)
    return h.reshape(bsz, seqlen, d)
```

```python
import functools

import jax
import jax.numpy as jnp
from jax import lax
from jax.experimental import pallas as pl
from jax.experimental.pallas import tpu as pltpu

F32 = jnp.float32
BF16 = jnp.bfloat16
EPS = 1e-6

V7X_LANES = 128
V7X_SUBLANES = 8
V7X_VMEM_BYTES = 64 * 1024 * 1024

SSM_GROUP = 16
SSM_STATE = 64
SSM_BLOCK_GROUPS = 16
GMLP_CHUNK = 128


def _rms(x, g):
    ms = jnp.mean(x * x, axis=-1, keepdims=True)
    return x * lax.rsqrt(ms + EPS) * g


def _resident(shape):
    n = len(shape)
    return pl.BlockSpec(shape, lambda *_: (0,) * n, pipeline_mode=pl.Buffered(1))


def _ffn_kernel(x_ref, g_ref, wg_ref, wu_ref, wd_ref, o_ref, xn_ref):
    @pl.when(pl.program_id(1) == 0)
    def _():
        x = x_ref[...]
        xn_ref[...] = _rms(x, g_ref[...]).astype(BF16)
        o_ref[...] = x

    xn = xn_ref[...]
    gate = jnp.dot(xn, wg_ref[...], preferred_element_type=F32)
    up = jnp.dot(xn, wu_ref[...], preferred_element_type=F32)
    act = (gate * jax.nn.sigmoid(gate) * (0.5 * up)).astype(BF16)
    o_ref[...] += jnp.dot(act, wd_ref[...], preferred_element_type=F32)


def _ffn(x, g, wg, wu, wd, *, tm, tf):
    m, d = x.shape
    f = wg.shape[1]
    vmem = (2 * tm * d * 4) * 2 + tm * d * 2 + 3 * 2 * d * tf * 2 + 4 * tm * tf * 4
    return pl.pallas_call(
        _ffn_kernel,
        grid=(m // tm, f // tf),
        in_specs=[
            pl.BlockSpec((tm, d), lambda i, j: (i, 0)),
            pl.BlockSpec((1, d), lambda i, j: (0, 0)),
            pl.BlockSpec((d, tf), lambda i, j: (0, j)),
            pl.BlockSpec((d, tf), lambda i, j: (0, j)),
            pl.BlockSpec((tf, d), lambda i, j: (j, 0)),
        ],
        out_specs=pl.BlockSpec((tm, d), lambda i, j: (i, 0)),
        out_shape=jax.ShapeDtypeStruct((m, d), F32),
        scratch_shapes=[pltpu.VMEM((tm, d), BF16)],
        compiler_params=pltpu.CompilerParams(
            dimension_semantics=("parallel", "arbitrary"),
            vmem_limit_bytes=min(vmem + (8 << 20), V7X_VMEM_BYTES - (2 << 20)),
        ),
        name="ffn",
    )(x, g.reshape(1, d), wg, wu, wd)


def _zoh(logdt, a_re, a_im):
    dt = jnp.exp(logdt)
    lr = jnp.minimum(a_re, -1e-4)
    li = a_im
    mag = jnp.exp(lr * dt)
    ang = li * dt
    abr = mag * jnp.cos(ang)
    abi = mag * jnp.sin(ang)
    den = lr * lr + li * li
    xr = abr - 1.0
    xi = abi
    zr = (xr * lr + xi * li) / den
    zi = (xi * lr - xr * li) / den
    return abr, abi, zr, zi


def _s5_prep_kernel(ldt_ref, are_ref, aim_ref, ldtb_ref, areb_ref, aimb_ref, bre_ref, bim_ref, cim_ref,
                    abr_ref, abi_ref, bbr_ref, bbi_ref, cimn_ref):
    abr, abi, _, _ = _zoh(ldt_ref[...], are_ref[...], aim_ref[...])
    abr_ref[...] = abr
    abi_ref[...] = abi
    _, _, zr, zi = _zoh(ldtb_ref[...], areb_ref[...], aimb_ref[...])
    br = bre_ref[...]
    bi = bim_ref[...]
    bbr_ref[...] = zr * br - zi * bi
    bbi_ref[...] = zr * bi + zi * br
    cimn_ref[...] = -cim_ref[...]


def _s5_prep(log_dt, a_re, a_im, b_re, b_im, c_im):
    g, n = a_re.shape
    p = b_re.shape[2]
    small = (g * n // V7X_LANES, V7X_LANES)
    big = (g * p * n // V7X_LANES, V7X_LANES)

    def bc(a):
        return jnp.broadcast_to(a[:, None, :], (g, p, n)).reshape(big)

    ldt = jnp.broadcast_to(log_dt[:, None], (g, n))
    outs = pl.pallas_call(
        _s5_prep_kernel,
        out_shape=[jax.ShapeDtypeStruct(small, F32)] * 2 + [jax.ShapeDtypeStruct(big, F32)] * 3,
        name="s5_prep",
    )(ldt.reshape(small), a_re.reshape(small), a_im.reshape(small),
      bc(ldt), bc(a_re), bc(a_im),
      jnp.swapaxes(b_re, 1, 2).reshape(big), jnp.swapaxes(b_im, 1, 2).reshape(big), c_im.reshape(big))
    abr, abi, bbr, bbi, cimn = outs
    return (abr.reshape(g, n), abi.reshape(g, n),
            bbr.reshape(g, p, n), bbi.reshape(g, p, n), cimn.reshape(g, p, n))


def _block_diag_in(w):
    g, p, n = w.shape
    nb = g // SSM_BLOCK_GROUPS
    w4 = w.reshape(nb, SSM_BLOCK_GROUPS, p, n)
    eye = jnp.eye(SSM_BLOCK_GROUPS, dtype=w.dtype)
    return jnp.einsum("kipn,ij->kipjn", w4, eye).reshape(nb, SSM_BLOCK_GROUPS * p, SSM_BLOCK_GROUPS * n)


def _block_diag_out(w):
    g, p, n = w.shape
    nb = g // SSM_BLOCK_GROUPS
    w4 = w.reshape(nb, SSM_BLOCK_GROUPS, p, n)
    eye = jnp.eye(SSM_BLOCK_GROUPS, dtype=w.dtype)
    return jnp.einsum("kipn,ij->kinjp", w4, eye).reshape(nb, SSM_BLOCK_GROUPS * n, SSM_BLOCK_GROUPS * p)


def _s5_kernel(h_ref, gmix_ref, win_ref, bblk_ref, cblk_ref, abr_ref, abi_ref, d_ref, wglu_ref, gout_ref,
               o_ref, dre0_ref, dim0_ref, u0_ref, dre1_ref, dim1_ref, u1_ref, st_ref,
               *, tc, nblk, bch, bst, chunks_per_seq):
    ntile = bst // V7X_LANES
    nb = tc // V7X_SUBLANES
    step_id = pl.program_id(0)
    bufs = ((dre0_ref, dim0_ref, u0_ref), (dre1_ref, dim1_ref, u1_ref))

    @pl.when(step_id == 0)
    def _():
        st_ref[...] = jnp.zeros_like(st_ref)
        for ref in bufs[1]:
            ref[...] = jnp.zeros_like(ref)

    def step(wr, rd):
        dre_w, dim_w, u_w = wr
        dre_r, dim_r, u_r = rd

        keep = jnp.where((step_id - 1) % chunks_per_seq == 0, 0.0, 1.0).astype(F32)
        ar = [abr_ref[k] for k in range(nblk)]
        ai = [abi_ref[k] for k in range(nblk)]
        s = [st_ref[i] * keep for i in range(2 * nblk)]
        for tb in range(nb):
            for r in range(V7X_SUBLANES):
                rows = slice(r * ntile, (r + 1) * ntile)
                for k in range(nblk):
                    sre, sim = s[2 * k], s[2 * k + 1]
                    nre = ar[k] * sre - ai[k] * sim + dre_r[k, tb, rows, :]
                    nim = ar[k] * sim + ai[k] * sre + dim_r[k, tb, rows, :]
                    dre_r[k, tb, rows, :] = nre
                    dim_r[k, tb, rows, :] = nim
                    s[2 * k], s[2 * k + 1] = nre, nim
        for i in range(2 * nblk):
            st_ref[i] = s[i]

        hn = _rms(h_ref[0], gmix_ref[...]).astype(BF16)
        u = jnp.dot(hn, win_ref[...], preferred_element_type=F32)
        u_w[...] = u
        ub = u.astype(BF16)
        for k in range(nblk):
            drv = jnp.dot(ub[:, k * bch:(k + 1) * bch], bblk_ref[k], preferred_element_type=F32)
            for j in range(ntile):
                lo = j * V7X_LANES
                tile = pl.ds(j, V7X_SUBLANES, stride=ntile)
                dre_w[k, :, tile, :] = drv[:, lo:lo + V7X_LANES].reshape(nb, V7X_SUBLANES, V7X_LANES)
                dim_w[k, :, tile, :] = drv[:, bst + lo:bst + lo + V7X_LANES].reshape(nb, V7X_SUBLANES, V7X_LANES)

        ys = []
        for k in range(nblk):
            parts = []
            for ref in (dre_r, dim_r):
                for j in range(ntile):
                    tile = pl.ds(j, V7X_SUBLANES, stride=ntile)
                    parts.append(ref[k, :, tile, :].reshape(tc, V7X_LANES))
            lhs = jnp.concatenate(parts, axis=1).astype(BF16)
            ys.append(jnp.dot(lhs, cblk_ref[k], preferred_element_type=F32))
        y = jnp.concatenate(ys, axis=1) + d_ref[...] * u_r[...]
        y = jax.nn.gelu(y)
        y = y * jax.nn.sigmoid(jnp.dot(y.astype(BF16), wglu_ref[...], preferred_element_type=F32))
        o_ref[0] = _rms(y, gout_ref[...]).astype(BF16)

    @pl.when(step_id % 2 == 0)
    def _():
        step(bufs[0], bufs[1])

    @pl.when(step_id % 2 == 1)
    def _():
        step(bufs[1], bufs[0])


def _s5(h3, gmix, win, bblk, cblk, abr, abi, dskip, wglu, gout, *, tc):
    b, l, d = h3.shape
    dssm = win.shape[1]
    nblk, bch, bst2 = bblk.shape
    bst = bst2 // 2
    ntile = bst // V7X_LANES
    assert ntile == V7X_SUBLANES and tc % V7X_SUBLANES == 0
    nb = tc // V7X_SUBLANES
    nch = l // tc
    last = b * nch - 1
    kern = functools.partial(_s5_kernel, tc=tc, nblk=nblk, bch=bch, bst=bst, chunks_per_seq=nch)
    dbuf = pltpu.VMEM((nblk, nb, ntile * V7X_SUBLANES, V7X_LANES), F32)
    ubuf = pltpu.VMEM((tc, dssm), F32)

    def in_chunk(s):
        c = jnp.minimum(s, last)
        return (c // nch, c % nch, 0)

    def out_chunk(s):
        c = jnp.maximum(s - 1, 0)
        return (c // nch, c % nch, 0)

    return pl.pallas_call(
        kern,
        grid=(b * nch + 1,),
        in_specs=[
            pl.BlockSpec((1, tc, d), in_chunk),
            _resident((1, d)),
            _resident((d, dssm)),
            _resident(bblk.shape),
            _resident(cblk.shape),
            _resident(abr.shape),
            _resident(abi.shape),
            _resident((1, dssm)),
            _resident(wglu.shape),
            _resident((1, dssm)),
        ],
        out_specs=pl.BlockSpec((1, tc, dssm), out_chunk),
        out_shape=jax.ShapeDtypeStruct((b, l, dssm), BF16),
        scratch_shapes=[
            dbuf, dbuf, ubuf, dbuf, dbuf, ubuf,
            pltpu.VMEM((2 * nblk, V7X_SUBLANES, V7X_LANES), F32),
        ],
        compiler_params=pltpu.CompilerParams(
            dimension_semantics=("arbitrary",),
            vmem_limit_bytes=56 << 20,
        ),
        name="s5",
    )(h3, gmix.reshape(1, d), win, bblk, cblk, abr, abi, dskip.reshape(1, dssm), wglu, gout.reshape(1, dssm))


def _gmlp_kernel(h_ref, gmix_ref, wuv_ref, gv_ref, ws_ref, bs_ref, gout_ref, o_ref, *, tm, dg):
    hn = _rms(h_ref[...], gmix_ref[...]).astype(BF16)
    zu = jnp.dot(hn, wuv_ref[:, :dg], preferred_element_type=F32)
    zv = jnp.dot(hn, wuv_ref[:, dg:], preferred_element_type=F32)
    u = jax.nn.gelu(zu)
    v = jax.nn.gelu(zv)
    vc = v - jnp.mean(v, axis=-1, keepdims=True)
    vn = vc * lax.rsqrt(jnp.mean(vc * vc, axis=-1, keepdims=True) + EPS) * gv_ref[...]
    vb = vn.astype(BF16)

    nh = dg // GMLP_CHUNK
    t_idx = lax.broadcasted_iota(jnp.int32, (GMLP_CHUNK, GMLP_CHUNK), 0)
    s_idx = lax.broadcasted_iota(jnp.int32, (GMLP_CHUNK, GMLP_CHUNK), 1)
    causal = t_idx >= s_idx
    wm = [jnp.where(causal, ws_ref[hd], 0.0).astype(BF16) for hd in range(nh)]
    rows = []
    for c in range(tm // GMLP_CHUNK):
        r0 = c * GMLP_CHUNK
        cols = []
        for hd in range(nh):
            c0 = hd * GMLP_CHUNK
            s = jnp.dot(wm[hd], vb[r0:r0 + GMLP_CHUNK, c0:c0 + GMLP_CHUNK], preferred_element_type=F32)
            cols.append(s + bs_ref[hd])
        rows.append(jnp.concatenate(cols, axis=1))
    yg = u * jnp.concatenate(rows, axis=0)
    o_ref[...] = _rms(yg, gout_ref[...]).astype(BF16)


def _gmlp(h, gmix, wuv, gv, ws, bs_full, gout, *, tm):
    m, d = h.shape
    dg = wuv.shape[1] // 2
    kern = functools.partial(_gmlp_kernel, tm=tm, dg=dg)
    return pl.pallas_call(
        kern,
        grid=(m // tm,),
        in_specs=[
            pl.BlockSpec((tm, d), lambda i: (i, 0)),
            _resident((1, d)),
            _resident(wuv.shape),
            _resident((1, dg)),
            _resident(ws.shape),
            _resident(bs_full.shape),
            _resident((1, dg)),
        ],
        out_specs=pl.BlockSpec((tm, dg), lambda i: (i, 0)),
        out_shape=jax.ShapeDtypeStruct((m, dg), BF16),
        compiler_params=pltpu.CompilerParams(
            dimension_semantics=("parallel",),
            vmem_limit_bytes=48 << 20,
        ),
        name="gmlp",
    )(h, gmix.reshape(1, d), wuv, gv.reshape(1, dg), ws, bs_full, gout.reshape(1, dg))


def _outproj_kernel(h_ref, ys_ref, yg_ref, wo_ref, o_ref, *, ds):
    acc = jnp.dot(ys_ref[...], wo_ref[:ds, :], preferred_element_type=F32)
    acc += jnp.dot(yg_ref[...], wo_ref[ds:, :], preferred_element_type=F32)
    o_ref[...] = h_ref[...] + acc


def _outproj(h, ys, yg, wo, *, tm):
    m, d = h.shape
    ds = ys.shape[1]
    dg = yg.shape[1]
    kern = functools.partial(_outproj_kernel, ds=ds)
    return pl.pallas_call(
        kern,
        grid=(m // tm,),
        in_specs=[
            pl.BlockSpec((tm, d), lambda i: (i, 0)),
            pl.BlockSpec((tm, ds), lambda i: (i, 0)),
            pl.BlockSpec((tm, dg), lambda i: (i, 0)),
            _resident(wo.shape),
        ],
        out_specs=pl.BlockSpec((tm, d), lambda i: (i, 0)),
        out_shape=jax.ShapeDtypeStruct((m, d), F32),
        compiler_params=pltpu.CompilerParams(
            dimension_semantics=("parallel",),
            vmem_limit_bytes=48 << 20,
        ),
        name="outproj",
    )(h, ys, yg, wo)


def _ple_kernel(h_ref, p_ref, gple_ref, wg_ref, wp_ref, gfin_ref, o_ref):
    h = h_ref[...]
    hn = _rms(h, gple_ref[...]).astype(BF16)
    gate = jax.nn.sigmoid(jnp.dot(hn, wg_ref[...], preferred_element_type=F32))
    pp = jnp.dot(p_ref[...].astype(BF16), wp_ref[...], preferred_element_type=F32)
    o_ref[...] = _rms(h + gate * pp, gfin_ref[...])


def _ple(h, p, gple, wg, wp, gfin, *, tm):
    m, d = h.shape
    dp = p.shape[1]
    return pl.pallas_call(
        _ple_kernel,
        grid=(m // tm,),
        in_specs=[
            pl.BlockSpec((tm, d), lambda i: (i, 0)),
            pl.BlockSpec((tm, dp), lambda i: (i, 0)),
            _resident((1, d)),
            _resident(wg.shape),
            _resident(wp.shape),
            _resident((1, d)),
        ],
        out_specs=pl.BlockSpec((tm, d), lambda i: (i, 0)),
        out_shape=jax.ShapeDtypeStruct((m, d), F32),
        compiler_params=pltpu.CompilerParams(
            dimension_semantics=("parallel",),
            vmem_limit_bytes=48 << 20,
        ),
        name="ple",
    )(h, p, gple.reshape(1, d), wg, wp, gfin.reshape(1, d))


def kernel(x, p, norm_ffn1, w1_gate, w1_up, w1_down, norm_mix, w_in, ssm_log_dt, ssm_a_re, ssm_a_im, ssm_b_re, ssm_b_im, ssm_c_re, ssm_c_im, ssm_d, ssm_w_glu, gmlp_norm_v, gmlp_w_s, gmlp_b_s, norm_ssm_out, norm_gmlp_out, w_out, norm_ffn2, w2_gate, w2_up, w2_down, norm_ple, w_ple_gate, w_ple_proj, norm_final):
    bsz, seqlen, d = x.shape
    depth = p.shape[0]
    m = bsz * seqlen
    dssm = ssm_d.shape[1]
    h = x.reshape(m, d)
    for i in range(depth):
        h = _ffn(h, norm_ffn1[i], w1_gate[i].astype(BF16), w1_up[i].astype(BF16), w1_down[i].astype(BF16),
                 tm=1024, tf=512)

        abr, abi, bbr, bbi, cimn = _s5_prep(ssm_log_dt[i], ssm_a_re[i], ssm_a_im[i],
                                            ssm_b_re[i], ssm_b_im[i], ssm_c_im[i])
        bblk = jnp.concatenate([_block_diag_in(bbr), _block_diag_in(bbi)], axis=2).astype(BF16)
        cblk = jnp.concatenate([_block_diag_out(ssm_c_re[i]), _block_diag_out(cimn)], axis=1).astype(BF16)
        nblk = bblk.shape[0]
        ab_shape = (nblk, V7X_SUBLANES, V7X_LANES)
        w_in_b = w_in[i].astype(BF16)
        ys = _s5(h.reshape(bsz, seqlen, d), norm_mix[i], w_in_b[:, :dssm], bblk, cblk,
                 abr.reshape(ab_shape), abi.reshape(ab_shape), ssm_d[i], ssm_w_glu[i].astype(BF16),
                 norm_ssm_out[i], tc=256)

        nh, ck = gmlp_b_s.shape[1:]
        bs_full = jnp.broadcast_to(gmlp_b_s[i][:, :, None], (nh, ck, ck))
        yg = _gmlp(h, norm_mix[i], w_in_b[:, dssm:], gmlp_norm_v[i], gmlp_w_s[i], bs_full,
                   norm_gmlp_out[i], tm=512)

        h = _outproj(h, ys.reshape(m, dssm), yg, w_out[i].astype(BF16), tm=512)

        h = _ffn(h, norm_ffn2[i], w2_gate[i].astype(BF16), w2_up[i].astype(BF16), w2_down[i].astype(BF16),
                 tm=1024, tf=512)

        norm_out = norm_final if i == depth - 1 else None
        assert norm_out is not None, "per-layer embedding kernel fuses the final norm; depth must be 1"
        h = _ple(h, p[i].reshape(m, -1), norm_ple[i], w_ple_gate[i].astype(BF16), w_ple_proj[i].astype(BF16),
                 norm_out, tm=512)
    return h.reshape(bsz, seqlen, d)
```

```python
import functools

import jax
import jax.numpy as jnp
from jax import lax
from jax.experimental import pallas as pl
from jax.experimental.pallas import tpu as pltpu

F32 = jnp.float32
BF16 = jnp.bfloat16
EPS = 1e-6

V7X_LANES = 128
V7X_SUBLANES = 8
V7X_VMEM_BYTES = 64 * 1024 * 1024

SSM_GROUP = 16
SSM_STATE = 64
SSM_BLOCK_GROUPS = 16
GMLP_CHUNK = 128
SCAN_ROW_PITCH = 12


def _rms(x, g):
    ms = jnp.mean(x * x, axis=-1, keepdims=True)
    return x * lax.rsqrt(ms + EPS) * g


def _resident(shape):
    n = len(shape)
    return pl.BlockSpec(shape, lambda *_: (0,) * n, pipeline_mode=pl.Buffered(1))


def _ffn_kernel(*refs, emit_norm):
    if emit_norm:
        x_ref, g_ref, wg_ref, wu_ref, wd_ref, gn_ref, o_ref, on_ref, xn_ref = refs
    else:
        x_ref, g_ref, wg_ref, wu_ref, wd_ref, o_ref, xn_ref = refs

    @pl.when(pl.program_id(1) == 0)
    def _():
        x = x_ref[...]
        xn_ref[...] = _rms(x, g_ref[...]).astype(BF16)
        o_ref[...] = x

    xn = xn_ref[...]
    gate = jnp.dot(xn, wg_ref[...], preferred_element_type=F32)
    up = jnp.dot(xn, wu_ref[...], preferred_element_type=F32)
    act = (gate * jax.nn.sigmoid(gate) * (0.5 * up)).astype(BF16)
    o_ref[...] += jnp.dot(act, wd_ref[...], preferred_element_type=F32)

    if emit_norm:
        @pl.when(pl.program_id(1) == pl.num_programs(1) - 1)
        def _():
            on_ref[...] = _rms(o_ref[...], gn_ref[...]).astype(BF16)


def _ffn(x, g, wg, wu, wd, g_next=None, *, tm, tf):
    m, d = x.shape
    f = wg.shape[1]
    emit_norm = g_next is not None
    row = pl.BlockSpec((tm, d), lambda i, j: (i, 0))
    vec = pl.BlockSpec((1, d), lambda i, j: (0, 0))
    vmem = (2 * tm * d * 4) * 2 + tm * d * 2 + 3 * 2 * d * tf * 2 + 4 * tm * tf * 4
    vmem += 2 * tm * d * 2 if emit_norm else 0
    out = pl.pallas_call(
        functools.partial(_ffn_kernel, emit_norm=emit_norm),
        grid=(m // tm, f // tf),
        in_specs=[
            row,
            vec,
            pl.BlockSpec((d, tf), lambda i, j: (0, j)),
            pl.BlockSpec((d, tf), lambda i, j: (0, j)),
            pl.BlockSpec((tf, d), lambda i, j: (j, 0)),
        ] + ([vec] if emit_norm else []),
        out_specs=[row, row] if emit_norm else row,
        out_shape=([jax.ShapeDtypeStruct((m, d), F32), jax.ShapeDtypeStruct((m, d), BF16)] if emit_norm
                   else jax.ShapeDtypeStruct((m, d), F32)),
        scratch_shapes=[pltpu.VMEM((tm, d), BF16)],
        compiler_params=pltpu.CompilerParams(
            dimension_semantics=("parallel", "arbitrary"),
            vmem_limit_bytes=min(vmem + (8 << 20), V7X_VMEM_BYTES - (1 << 20)),
        ),
        name="ffn",
    )(*((x, g.reshape(1, d), wg, wu, wd) + ((g_next.reshape(1, d),) if emit_norm else ())))
    return out


def _zoh(logdt, a_re, a_im):
    dt = jnp.exp(logdt)
    lr = jnp.minimum(a_re, -1e-4)
    li = a_im
    mag = jnp.exp(lr * dt)
    ang = li * dt
    abr = mag * jnp.cos(ang)
    abi = mag * jnp.sin(ang)
    den = lr * lr + li * li
    xr = abr - 1.0
    xi = abi
    zr = (xr * lr + xi * li) / den
    zi = (xi * lr - xr * li) / den
    return abr, abi, zr, zi


def _s5_prep_kernel(ldt_ref, are_ref, aim_ref, ldtb_ref, areb_ref, aimb_ref, bre_ref, bim_ref, cim_ref,
                    abr_ref, abi_ref, bbr_ref, bbi_ref, cimn_ref):
    abr, abi, _, _ = _zoh(ldt_ref[...], are_ref[...], aim_ref[...])
    abr_ref[...] = abr
    abi_ref[...] = abi
    _, _, zr, zi = _zoh(ldtb_ref[...], areb_ref[...], aimb_ref[...])
    br = bre_ref[...]
    bi = bim_ref[...]
    bbr_ref[...] = zr * br - zi * bi
    bbi_ref[...] = zr * bi + zi * br
    cimn_ref[...] = -cim_ref[...]


def _s5_prep(log_dt, a_re, a_im, b_re, b_im, c_im):
    g, n = a_re.shape
    p = b_re.shape[2]
    small = (g * n // V7X_LANES, V7X_LANES)
    big = (g * p * n // V7X_LANES, V7X_LANES)

    def bc(a):
        return jnp.broadcast_to(a[:, None, :], (g, p, n)).reshape(big)

    ldt = jnp.broadcast_to(log_dt[:, None], (g, n))
    outs = pl.pallas_call(
        _s5_prep_kernel,
        out_shape=[jax.ShapeDtypeStruct(small, F32)] * 2 + [jax.ShapeDtypeStruct(big, F32)] * 3,
        name="s5_prep",
    )(ldt.reshape(small), a_re.reshape(small), a_im.reshape(small),
      bc(ldt), bc(a_re), bc(a_im),
      jnp.swapaxes(b_re, 1, 2).reshape(big), jnp.swapaxes(b_im, 1, 2).reshape(big), c_im.reshape(big))
    abr, abi, bbr, bbi, cimn = outs
    return (abr.reshape(g, n), abi.reshape(g, n),
            bbr.reshape(g, p, n), bbi.reshape(g, p, n), cimn.reshape(g, p, n))


def _block_diag_in(w):
    g, p, n = w.shape
    nb = g // SSM_BLOCK_GROUPS
    w4 = w.reshape(nb, SSM_BLOCK_GROUPS, p, n)
    eye = jnp.eye(SSM_BLOCK_GROUPS, dtype=w.dtype)
    return jnp.einsum("kipn,ij->kipjn", w4, eye).reshape(nb, SSM_BLOCK_GROUPS * p, SSM_BLOCK_GROUPS * n)


def _block_diag_out(w):
    g, p, n = w.shape
    nb = g // SSM_BLOCK_GROUPS
    w4 = w.reshape(nb, SSM_BLOCK_GROUPS, p, n)
    eye = jnp.eye(SSM_BLOCK_GROUPS, dtype=w.dtype)
    return jnp.einsum("kipn,ij->kinjp", w4, eye).reshape(nb, SSM_BLOCK_GROUPS * n, SSM_BLOCK_GROUPS * p)


def _s5_kernel(hn_ref, win_ref, bblk_ref, cblk_ref, abr_ref, abi_ref, d_ref, wglu_ref, gout_ref,
               o_ref, dre0_ref, dim0_ref, u0_ref, dre1_ref, dim1_ref, u1_ref, st_ref,
               *, tc, nblk, bch, bst, chunks_per_seq):
    ntile = bst // V7X_LANES
    nb = tc // V7X_SUBLANES
    step_id = pl.program_id(0)
    bufs = ((dre0_ref, dim0_ref, u0_ref), (dre1_ref, dim1_ref, u1_ref))

    @pl.when(step_id == 0)
    def _():
        st_ref[...] = jnp.zeros_like(st_ref)
        for ref in bufs[1]:
            ref[...] = jnp.zeros_like(ref)

    def step(wr, rd):
        dre_w, dim_w, u_w = wr
        dre_r, dim_r, u_r = rd

        keep = jnp.where((step_id - 1) % chunks_per_seq == 0, 0.0, 1.0).astype(F32)
        ar = [abr_ref[k] for k in range(nblk)]
        ai = [abi_ref[k] for k in range(nblk)]
        s = [st_ref[i] * keep for i in range(2 * nblk)]
        for tb in range(nb):
            for r in range(V7X_SUBLANES):
                rows = slice(r * SCAN_ROW_PITCH, r * SCAN_ROW_PITCH + ntile)
                for k in range(nblk):
                    sre, sim = s[2 * k], s[2 * k + 1]
                    nre = ar[k] * sre - ai[k] * sim + dre_r[k, tb, rows, :]
                    nim = ar[k] * sim + ai[k] * sre + dim_r[k, tb, rows, :]
                    dre_r[k, tb, rows, :] = nre
                    dim_r[k, tb, rows, :] = nim
                    s[2 * k], s[2 * k + 1] = nre, nim
        for i in range(2 * nblk):
            st_ref[i] = s[i]

        u = jnp.dot(hn_ref[0], win_ref[...], preferred_element_type=F32)
        u_w[...] = u
        ub = u.astype(BF16)
        for k in range(nblk):
            drv = jnp.dot(ub[:, k * bch:(k + 1) * bch], bblk_ref[k], preferred_element_type=F32)
            for j in range(ntile):
                lo = j * V7X_LANES
                tile = pl.ds(j, V7X_SUBLANES, stride=SCAN_ROW_PITCH)
                dre_w[k, :, tile, :] = drv[:, lo:lo + V7X_LANES].reshape(nb, V7X_SUBLANES, V7X_LANES)
                dim_w[k, :, tile, :] = drv[:, bst + lo:bst + lo + V7X_LANES].reshape(nb, V7X_SUBLANES, V7X_LANES)

        ys = []
        for k in range(nblk):
            parts = []
            for ref in (dre_r, dim_r):
                for j in range(ntile):
                    tile = pl.ds(j, V7X_SUBLANES, stride=SCAN_ROW_PITCH)
                    parts.append(ref[k, :, tile, :].reshape(tc, V7X_LANES))
            lhs = jnp.concatenate(parts, axis=1).astype(BF16)
            ys.append(jnp.dot(lhs, cblk_ref[k], preferred_element_type=F32))
        y = jnp.concatenate(ys, axis=1) + d_ref[...] * u_r[...]
        y = jax.nn.gelu(y)
        y = y * jax.nn.sigmoid(jnp.dot(y.astype(BF16), wglu_ref[...], preferred_element_type=F32))
        o_ref[0] = _rms(y, gout_ref[...]).astype(BF16)

    @pl.when(step_id % 2 == 0)
    def _():
        step(bufs[0], bufs[1])

    @pl.when(step_id % 2 == 1)
    def _():
        step(bufs[1], bufs[0])


def _s5(hn3, win, bblk, cblk, abr, abi, dskip, wglu, gout, *, tc):
    b, l, d = hn3.shape
    dssm = dskip.shape[0]
    nblk, bch, bst2 = bblk.shape
    bst = bst2 // 2
    ntile = bst // V7X_LANES
    assert ntile == V7X_SUBLANES and tc % V7X_SUBLANES == 0
    nb = tc // V7X_SUBLANES
    nch = l // tc
    last = b * nch - 1
    kern = functools.partial(_s5_kernel, tc=tc, nblk=nblk, bch=bch, bst=bst, chunks_per_seq=nch)
    dbuf = pltpu.VMEM((nblk, nb, V7X_SUBLANES * SCAN_ROW_PITCH, V7X_LANES), F32)
    ubuf = pltpu.VMEM((tc, dssm), F32)

    def in_chunk(s):
        c = jnp.minimum(s, last)
        return (c // nch, c % nch, 0)

    def out_chunk(s):
        c = jnp.maximum(s - 1, 0)
        return (c // nch, c % nch, 0)

    return pl.pallas_call(
        kern,
        grid=(b * nch + 1,),
        in_specs=[
            pl.BlockSpec((1, tc, d), in_chunk),
            pl.BlockSpec((d, dssm), lambda s: (0, 0), pipeline_mode=pl.Buffered(1)),
            _resident(bblk.shape),
            _resident(cblk.shape),
            _resident(abr.shape),
            _resident(abi.shape),
            _resident((1, dssm)),
            _resident(wglu.shape),
            _resident((1, dssm)),
        ],
        out_specs=pl.BlockSpec((1, tc, dssm), out_chunk),
        out_shape=jax.ShapeDtypeStruct((b, l, dssm), BF16),
        scratch_shapes=[
            dbuf, dbuf, ubuf, dbuf, dbuf, ubuf,
            pltpu.VMEM((2 * nblk, V7X_SUBLANES, V7X_LANES), F32),
        ],
        compiler_params=pltpu.CompilerParams(
            dimension_semantics=("arbitrary",),
            vmem_limit_bytes=56 << 20,
        ),
        name="s5",
    )(hn3, win, bblk, cblk, abr, abi, dskip.reshape(1, dssm), wglu, gout.reshape(1, dssm))


def _gmlp_kernel(hn_ref, wu_ref, wv_ref, gv_ref, ws_ref, bs_ref, gout_ref, o_ref, *, tm, dg):
    hn = hn_ref[...]
    zu = jnp.dot(hn, wu_ref[...], preferred_element_type=F32)
    zv = jnp.dot(hn, wv_ref[...], preferred_element_type=F32)
    u = jax.nn.gelu(zu)
    v = jax.nn.gelu(zv)
    vc = v - jnp.mean(v, axis=-1, keepdims=True)
    vn = vc * lax.rsqrt(jnp.mean(vc * vc, axis=-1, keepdims=True) + EPS) * gv_ref[...]
    vb = vn.astype(BF16)

    nh = dg // GMLP_CHUNK
    t_idx = lax.broadcasted_iota(jnp.int32, (GMLP_CHUNK, GMLP_CHUNK), 0)
    s_idx = lax.broadcasted_iota(jnp.int32, (GMLP_CHUNK, GMLP_CHUNK), 1)
    causal = t_idx >= s_idx
    wm = [jnp.where(causal, ws_ref[hd], 0.0).astype(BF16) for hd in range(nh)]
    rows = []
    for c in range(tm // GMLP_CHUNK):
        r0 = c * GMLP_CHUNK
        cols = []
        for hd in range(nh):
            c0 = hd * GMLP_CHUNK
            s = jnp.dot(wm[hd], vb[r0:r0 + GMLP_CHUNK, c0:c0 + GMLP_CHUNK], preferred_element_type=F32)
            cols.append(s + bs_ref[hd])
        rows.append(jnp.concatenate(cols, axis=1))
    yg = u * jnp.concatenate(rows, axis=0)
    o_ref[...] = _rms(yg, gout_ref[...]).astype(BF16)


def _gmlp(hn, win, gv, ws, bs_full, gout, *, tm):
    m, d = hn.shape
    dg = gv.shape[0]
    ublk = (win.shape[1] - 2 * dg) // dg
    kern = functools.partial(_gmlp_kernel, tm=tm, dg=dg)
    return pl.pallas_call(
        kern,
        grid=(m // tm,),
        in_specs=[
            pl.BlockSpec((tm, d), lambda i: (i, 0)),
            pl.BlockSpec((d, dg), lambda i: (0, ublk), pipeline_mode=pl.Buffered(1)),
            pl.BlockSpec((d, dg), lambda i: (0, ublk + 1), pipeline_mode=pl.Buffered(1)),
            _resident((1, dg)),
            _resident(ws.shape),
            _resident(bs_full.shape),
            _resident((1, dg)),
        ],
        out_specs=pl.BlockSpec((tm, dg), lambda i: (i, 0)),
        out_shape=jax.ShapeDtypeStruct((m, dg), BF16),
        compiler_params=pltpu.CompilerParams(
            dimension_semantics=("parallel",),
            vmem_limit_bytes=48 << 20,
        ),
        name="gmlp",
    )(hn, win, win, gv.reshape(1, dg), ws, bs_full, gout.reshape(1, dg))


def _outproj_kernel(h_ref, ys_ref, yg_ref, wo_ref, o_ref, *, ds):
    acc = jnp.dot(ys_ref[...], wo_ref[:ds, :], preferred_element_type=F32)
    acc += jnp.dot(yg_ref[...], wo_ref[ds:, :], preferred_element_type=F32)
    o_ref[...] = h_ref[...] + acc


def _outproj(h, ys, yg, wo, *, tm):
    m, d = h.shape
    ds = ys.shape[1]
    dg = yg.shape[1]
    kern = functools.partial(_outproj_kernel, ds=ds)
    return pl.pallas_call(
        kern,
        grid=(m // tm,),
        in_specs=[
            pl.BlockSpec((tm, d), lambda i: (i, 0)),
            pl.BlockSpec((tm, ds), lambda i: (i, 0)),
            pl.BlockSpec((tm, dg), lambda i: (i, 0)),
            _resident(wo.shape),
        ],
        out_specs=pl.BlockSpec((tm, d), lambda i: (i, 0)),
        out_shape=jax.ShapeDtypeStruct((m, d), F32),
        compiler_params=pltpu.CompilerParams(
            dimension_semantics=("parallel",),
            vmem_limit_bytes=48 << 20,
        ),
        name="outproj",
    )(h, ys, yg, wo)


def _ple_kernel(h_ref, p_ref, gple_ref, wg_ref, wp_ref, gfin_ref, o_ref):
    h = h_ref[...]
    hn = _rms(h, gple_ref[...]).astype(BF16)
    gate = jax.nn.sigmoid(jnp.dot(hn, wg_ref[...], preferred_element_type=F32))
    pp = jnp.dot(p_ref[...].astype(BF16), wp_ref[...], preferred_element_type=F32)
    o_ref[...] = _rms(h + gate * pp, gfin_ref[...])


def _ple(h, p, gple, wg, wp, gfin, *, tm):
    m, d = h.shape
    dp = p.shape[1]
    return pl.pallas_call(
        _ple_kernel,
        grid=(m // tm,),
        in_specs=[
            pl.BlockSpec((tm, d), lambda i: (i, 0)),
            pl.BlockSpec((tm, dp), lambda i: (i, 0)),
            _resident((1, d)),
            _resident(wg.shape),
            _resident(wp.shape),
            _resident((1, d)),
        ],
        out_specs=pl.BlockSpec((tm, d), lambda i: (i, 0)),
        out_shape=jax.ShapeDtypeStruct((m, d), F32),
        compiler_params=pltpu.CompilerParams(
            dimension_semantics=("parallel",),
            vmem_limit_bytes=48 << 20,
        ),
        name="ple",
    )(h, p, gple.reshape(1, d), wg, wp, gfin.reshape(1, d))


def kernel(x, p, norm_ffn1, w1_gate, w1_up, w1_down, norm_mix, w_in, ssm_log_dt, ssm_a_re, ssm_a_im, ssm_b_re, ssm_b_im, ssm_c_re, ssm_c_im, ssm_d, ssm_w_glu, gmlp_norm_v, gmlp_w_s, gmlp_b_s, norm_ssm_out, norm_gmlp_out, w_out, norm_ffn2, w2_gate, w2_up, w2_down, norm_ple, w_ple_gate, w_ple_proj, norm_final):
    bsz, seqlen, d = x.shape
    depth = p.shape[0]
    m = bsz * seqlen
    dssm = ssm_d.shape[1]
    h = x.reshape(m, d)
    for i in range(depth):
        h, hn = _ffn(h, norm_ffn1[i], w1_gate[i].astype(BF16), w1_up[i].astype(BF16), w1_down[i].astype(BF16),
                     norm_mix[i], tm=1024, tf=512)

        abr, abi, bbr, bbi, cimn = _s5_prep(ssm_log_dt[i], ssm_a_re[i], ssm_a_im[i],
                                            ssm_b_re[i], ssm_b_im[i], ssm_c_im[i])
        bblk = jnp.concatenate([_block_diag_in(bbr), _block_diag_in(bbi)], axis=2).astype(BF16)
        cblk = jnp.concatenate([_block_diag_out(ssm_c_re[i]), _block_diag_out(cimn)], axis=1).astype(BF16)
        nblk = bblk.shape[0]
        ab_shape = (nblk, V7X_SUBLANES, V7X_LANES)
        w_in_b = w_in[i].astype(BF16)
        ys = _s5(hn.reshape(bsz, seqlen, d), w_in_b, bblk, cblk,
                 abr.reshape(ab_shape), abi.reshape(ab_shape), ssm_d[i], ssm_w_glu[i].astype(BF16),
                 norm_ssm_out[i], tc=256)

        nh, ck = gmlp_b_s.shape[1:]
        bs_full = jnp.broadcast_to(gmlp_b_s[i][:, :, None], (nh, ck, ck))
        yg = _gmlp(hn, w_in_b, gmlp_norm_v[i], gmlp_w_s[i], bs_full, norm_gmlp_out[i], tm=512)

        h = _outproj(h, ys.reshape(m, dssm), yg, w_out[i].astype(BF16), tm=512)

        h = _ffn(h, norm_ffn2[i], w2_gate[i].astype(BF16), w2_up[i].astype(BF16), w2_down[i].astype(BF16),
                 tm=1024, tf=512)

        norm_out = norm_final if i == depth - 1 else None
        assert norm_out is not None, "per-layer embedding kernel fuses the final norm; depth must be 1"
        h = _ple(h, p[i].reshape(m, -1), norm_ple[i], w_ple_gate[i].astype(BF16), w_ple_proj[i].astype(BF16),
                 norm_out, tm=512)
    return h.reshape(bsz, seqlen, d)
```

```python
import functools
from typing import Callable, NamedTuple

import jax
import jax.numpy as jnp
from jax import lax
from jax.experimental import pallas as pl
from jax.experimental.pallas import tpu as pltpu

F32 = jnp.float32
BF16 = jnp.bfloat16
EPS = 1e-6

V7X_LANES = 128
V7X_SUBLANES = 8
V7X_VMEM_BYTES = 64 * 1024 * 1024

SSM_GROUP = 16
SSM_STATE = 64
SSM_BLOCK_GROUPS = 16
GMLP_CHUNK = 128
SCAN_ROW_PITCH = 12

FFN_TM = 1024
FFN_TF = 512
S5_TC = 256
ROW_TM = 512


def _rms(x, g):
    ms = jnp.mean(x * x, axis=-1, keepdims=True)
    return x * lax.rsqrt(ms + EPS) * g


def _resident(shape):
    n = len(shape)
    return pl.BlockSpec(shape, lambda *_: (0,) * n, pipeline_mode=pl.Buffered(1))


class _CastJob(NamedTuple):
    src: jax.Array
    block: tuple
    index_map: Callable
    first_only: bool = False


def _cast_specs(jobs):
    ins = [pl.BlockSpec(j.block, j.index_map, pipeline_mode=pl.Buffered(1)) if j.first_only
           else pl.BlockSpec(j.block, j.index_map) for j in jobs]
    outs = [pl.BlockSpec(j.block, j.index_map) for j in jobs]
    shapes = [jax.ShapeDtypeStruct(j.src.shape, BF16) for j in jobs]
    return ins, outs, shapes


def _run_casts(srcs, dsts, flags, want):
    for src, dst, first_only in zip(srcs, dsts, flags):
        if first_only == want:
            dst[...] = src[...].astype(BF16)


BF16_ROWS = 2 * V7X_SUBLANES


def _grid_tiles(w, nrow, nf, swap=False):
    r, c = w.shape
    if swap:
        assert r % nf == 0 and c % nrow == 0
        return _CastJob(w, (r // nf, c // nrow), lambda i, j: (j, i))
    assert r % nrow == 0 and c % nf == 0
    return _CastJob(w, (r // nrow, c // nf), lambda i, j: (i, j))


def _row_tiles(w, nrow):
    r, c = w.shape
    assert r % nrow == 0
    return _CastJob(w, (r // nrow, c), lambda i, j: (i, 0), first_only=True)


def _step_tiles(w, nsteps):
    r, c = w.shape
    rows = max(BF16_ROWS, r // (nsteps - 1))
    n = r // rows
    assert r % rows == 0 and rows % BF16_ROWS == 0 and n <= nsteps
    return _CastJob(w, (rows, c), lambda s: (jnp.minimum(s, n - 1), 0))


def _ffn_kernel(*refs, cast_first_only):
    n = len(cast_first_only)
    x_ref, g_ref, wg_ref, wu_ref, wd_ref = refs[:5]
    cast_src, o_ref, cast_dst, xn_ref = refs[5:5 + n], refs[5 + n], refs[6 + n:6 + 2 * n], refs[6 + 2 * n]

    @pl.when(pl.program_id(1) == 0)
    def _():
        x = x_ref[...]
        xn_ref[...] = _rms(x, g_ref[...]).astype(BF16)
        o_ref[...] = x
        _run_casts(cast_src, cast_dst, cast_first_only, True)

    _run_casts(cast_src, cast_dst, cast_first_only, False)
    xn = xn_ref[...]
    gate = jnp.dot(xn, wg_ref[...], preferred_element_type=F32)
    up = jnp.dot(xn, wu_ref[...], preferred_element_type=F32)
    act = (gate * jax.nn.sigmoid(gate) * (0.5 * up)).astype(BF16)
    o_ref[...] += jnp.dot(act, wd_ref[...], preferred_element_type=F32)


def _ffn(x, g, wg, wu, wd, casts=(), *, tm, tf):
    m, d = x.shape
    f = wg.shape[1]
    row = pl.BlockSpec((tm, d), lambda i, j: (i, 0))
    cast_in, cast_out, cast_shapes = _cast_specs(casts)
    vmem = (2 * tm * d * 4) * 2 + tm * d * 2 + 3 * 2 * d * tf * 2 + 4 * tm * tf * 4
    vmem += sum(c.block[0] * c.block[1] * (4 * (1 if c.first_only else 2) + 2 * 2) for c in casts)
    outs = pl.pallas_call(
        functools.partial(_ffn_kernel, cast_first_only=tuple(c.first_only for c in casts)),
        grid=(m // tm, f // tf),
        in_specs=[
            row,
            pl.BlockSpec((1, d), lambda i, j: (0, 0)),
            pl.BlockSpec((d, tf), lambda i, j: (0, j)),
            pl.BlockSpec((d, tf), lambda i, j: (0, j)),
            pl.BlockSpec((tf, d), lambda i, j: (j, 0)),
        ] + cast_in,
        out_specs=[row] + cast_out,
        out_shape=[jax.ShapeDtypeStruct((m, d), F32)] + cast_shapes,
        scratch_shapes=[pltpu.VMEM((tm, d), BF16)],
        compiler_params=pltpu.CompilerParams(
            dimension_semantics=("parallel", "arbitrary"),
            vmem_limit_bytes=min(vmem + (8 << 20), V7X_VMEM_BYTES - (2 << 20)),
        ),
        name="ffn",
    )(x, g.reshape(1, d), wg, wu, wd, *[c.src for c in casts])
    return outs[0], list(outs[1:])


def _zoh(logdt, a_re, a_im):
    dt = jnp.exp(logdt)
    lr = jnp.minimum(a_re, -1e-4)
    li = a_im
    mag = jnp.exp(lr * dt)
    ang = li * dt
    abr = mag * jnp.cos(ang)
    abi = mag * jnp.sin(ang)
    den = lr * lr + li * li
    xr = abr - 1.0
    xi = abi
    zr = (xr * lr + xi * li) / den
    zi = (xi * lr - xr * li) / den
    return abr, abi, zr, zi


def _s5_prep_kernel(ldt_ref, are_ref, aim_ref, ldtb_ref, areb_ref, aimb_ref, bre_ref, bim_ref, cim_ref,
                    abr_ref, abi_ref, bbr_ref, bbi_ref, cimn_ref):
    abr, abi, _, _ = _zoh(ldt_ref[...], are_ref[...], aim_ref[...])
    abr_ref[...] = abr
    abi_ref[...] = abi
    _, _, zr, zi = _zoh(ldtb_ref[...], areb_ref[...], aimb_ref[...])
    br = bre_ref[...]
    bi = bim_ref[...]
    bbr_ref[...] = zr * br - zi * bi
    bbi_ref[...] = zr * bi + zi * br
    cimn_ref[...] = -cim_ref[...]


def _s5_prep(log_dt, a_re, a_im, b_re, b_im, c_im):
    g, n = a_re.shape
    p = b_re.shape[2]
    small = (g * n // V7X_LANES, V7X_LANES)
    big = (g * p * n // V7X_LANES, V7X_LANES)

    def bc(a):
        return jnp.broadcast_to(a[:, None, :], (g, p, n)).reshape(big)

    ldt = jnp.broadcast_to(log_dt[:, None], (g, n))
    outs = pl.pallas_call(
        _s5_prep_kernel,
        out_shape=[jax.ShapeDtypeStruct(small, F32)] * 2 + [jax.ShapeDtypeStruct(big, F32)] * 3,
        name="s5_prep",
    )(ldt.reshape(small), a_re.reshape(small), a_im.reshape(small),
      bc(ldt), bc(a_re), bc(a_im),
      jnp.swapaxes(b_re, 1, 2).reshape(big), jnp.swapaxes(b_im, 1, 2).reshape(big), c_im.reshape(big))
    abr, abi, bbr, bbi, cimn = outs
    return (abr.reshape(g, n), abi.reshape(g, n),
            bbr.reshape(g, p, n), bbi.reshape(g, p, n), cimn.reshape(g, p, n))


def _block_diag_in(w):
    g, p, n = w.shape
    nb = g // SSM_BLOCK_GROUPS
    w4 = w.reshape(nb, SSM_BLOCK_GROUPS, p, n)
    eye = jnp.eye(SSM_BLOCK_GROUPS, dtype=w.dtype)
    return jnp.einsum("kipn,ij->kipjn", w4, eye).reshape(nb, SSM_BLOCK_GROUPS * p, SSM_BLOCK_GROUPS * n)


def _block_diag_out(w):
    g, p, n = w.shape
    nb = g // SSM_BLOCK_GROUPS
    w4 = w.reshape(nb, SSM_BLOCK_GROUPS, p, n)
    eye = jnp.eye(SSM_BLOCK_GROUPS, dtype=w.dtype)
    return jnp.einsum("kipn,ij->kinjp", w4, eye).reshape(nb, SSM_BLOCK_GROUPS * n, SSM_BLOCK_GROUPS * p)


def _s5_kernel(*refs, tc, nblk, bch, bst, chunks_per_seq, n_cast):
    h_ref, gmix_ref, win_ref, bblk_ref, cblk_ref, abr_ref, abi_ref, d_ref, wglu_ref, gout_ref = refs[:10]
    cast_src, o_ref, cast_dst = refs[10:10 + n_cast], refs[10 + n_cast], refs[11 + n_cast:11 + 2 * n_cast]
    dre0_ref, dim0_ref, u0_ref, dre1_ref, dim1_ref, u1_ref, st_ref = refs[11 + 2 * n_cast:]
    ntile = bst // V7X_LANES
    nb = tc // V7X_SUBLANES
    step_id = pl.program_id(0)
    bufs = ((dre0_ref, dim0_ref, u0_ref), (dre1_ref, dim1_ref, u1_ref))
    _run_casts(cast_src, cast_dst, (False,) * n_cast, False)

    @pl.when(step_id == 0)
    def _():
        st_ref[...] = jnp.zeros_like(st_ref)
        for ref in bufs[1]:
            ref[...] = jnp.zeros_like(ref)

    def step(wr, rd):
        dre_w, dim_w, u_w = wr
        dre_r, dim_r, u_r = rd

        keep = jnp.where((step_id - 1) % chunks_per_seq == 0, 0.0, 1.0).astype(F32)
        ar = [abr_ref[k] for k in range(nblk)]
        ai = [abi_ref[k] for k in range(nblk)]
        s = [st_ref[i] * keep for i in range(2 * nblk)]
        for tb in range(nb):
            for r in range(V7X_SUBLANES):
                rows = slice(r * SCAN_ROW_PITCH, r * SCAN_ROW_PITCH + ntile)
                for k in range(nblk):
                    sre, sim = s[2 * k], s[2 * k + 1]
                    nre = ar[k] * sre - ai[k] * sim + dre_r[k, tb, rows, :]
                    nim = ar[k] * sim + ai[k] * sre + dim_r[k, tb, rows, :]
                    dre_r[k, tb, rows, :] = nre
                    dim_r[k, tb, rows, :] = nim
                    s[2 * k], s[2 * k + 1] = nre, nim
        for i in range(2 * nblk):
            st_ref[i] = s[i]

        hn = _rms(h_ref[0], gmix_ref[...]).astype(BF16)
        u = jnp.dot(hn, win_ref[...], preferred_element_type=F32)
        u_w[...] = u
        ub = u.astype(BF16)
        for k in range(nblk):
            drv = jnp.dot(ub[:, k * bch:(k + 1) * bch], bblk_ref[k], preferred_element_type=F32)
            for j in range(ntile):
                lo = j * V7X_LANES
                tile = pl.ds(j, V7X_SUBLANES, stride=SCAN_ROW_PITCH)
                dre_w[k, :, tile, :] = drv[:, lo:lo + V7X_LANES].reshape(nb, V7X_SUBLANES, V7X_LANES)
                dim_w[k, :, tile, :] = drv[:, bst + lo:bst + lo + V7X_LANES].reshape(nb, V7X_SUBLANES, V7X_LANES)

        ys = []
        for k in range(nblk):
            parts = []
            for ref in (dre_r, dim_r):
                for j in range(ntile):
                    tile = pl.ds(j, V7X_SUBLANES, stride=SCAN_ROW_PITCH)
                    parts.append(ref[k, :, tile, :].reshape(tc, V7X_LANES))
            lhs = jnp.concatenate(parts, axis=1).astype(BF16)
            ys.append(jnp.dot(lhs, cblk_ref[k], preferred_element_type=F32))
        y = jnp.concatenate(ys, axis=1) + d_ref[...] * u_r[...]
        y = jax.nn.gelu(y)
        y = y * jax.nn.sigmoid(jnp.dot(y.astype(BF16), wglu_ref[...], preferred_element_type=F32))
        o_ref[0] = _rms(y, gout_ref[...]).astype(BF16)

    @pl.when(step_id % 2 == 0)
    def _():
        step(bufs[0], bufs[1])

    @pl.when(step_id % 2 == 1)
    def _():
        step(bufs[1], bufs[0])


def _s5(h3, gmix, win, bblk, cblk, abr, abi, dskip, wglu, gout, casts=(), *, tc):
    b, l, d = h3.shape
    dssm = dskip.shape[0]
    nblk, bch, bst2 = bblk.shape
    bst = bst2 // 2
    ntile = bst // V7X_LANES
    assert ntile == V7X_SUBLANES and tc % V7X_SUBLANES == 0
    nb = tc // V7X_SUBLANES
    nch = l // tc
    last = b * nch - 1
    kern = functools.partial(_s5_kernel, tc=tc, nblk=nblk, bch=bch, bst=bst, chunks_per_seq=nch,
                             n_cast=len(casts))
    cast_in, cast_out, cast_shapes = _cast_specs(casts)
    dbuf = pltpu.VMEM((nblk, nb, V7X_SUBLANES * SCAN_ROW_PITCH, V7X_LANES), F32)
    ubuf = pltpu.VMEM((tc, dssm), F32)

    def in_chunk(s):
        c = jnp.minimum(s, last)
        return (c // nch, c % nch, 0)

    def out_chunk(s):
        c = jnp.maximum(s - 1, 0)
        return (c // nch, c % nch, 0)

    outs = pl.pallas_call(
        kern,
        grid=(b * nch + 1,),
        in_specs=[
            pl.BlockSpec((1, tc, d), in_chunk),
            _resident((1, d)),
            pl.BlockSpec((d, dssm), lambda s: (0, 0), pipeline_mode=pl.Buffered(1)),
            _resident(bblk.shape),
            _resident(cblk.shape),
            _resident(abr.shape),
            _resident(abi.shape),
            _resident((1, dssm)),
            _resident(wglu.shape),
            _resident((1, dssm)),
        ] + cast_in,
        out_specs=[pl.BlockSpec((1, tc, dssm), out_chunk)] + cast_out,
        out_shape=[jax.ShapeDtypeStruct((b, l, dssm), BF16)] + cast_shapes,
        scratch_shapes=[
            dbuf, dbuf, ubuf, dbuf, dbuf, ubuf,
            pltpu.VMEM((2 * nblk, V7X_SUBLANES, V7X_LANES), F32),
        ],
        compiler_params=pltpu.CompilerParams(
            dimension_semantics=("arbitrary",),
            vmem_limit_bytes=56 << 20,
        ),
        name="s5",
    )(h3, gmix.reshape(1, d), win, bblk, cblk, abr, abi, dskip.reshape(1, dssm), wglu, gout.reshape(1, dssm),
      *[c.src for c in casts])
    return outs[0], list(outs[1:])


def _gmlp_kernel(h_ref, gmix_ref, wu_ref, wv_ref, gv_ref, ws_ref, bs_ref, gout_ref, o_ref, *, tm, dg):
    hn = _rms(h_ref[...], gmix_ref[...]).astype(BF16)
    zu = jnp.dot(hn, wu_ref[...], preferred_element_type=F32)
    zv = jnp.dot(hn, wv_ref[...], preferred_element_type=F32)
    u = jax.nn.gelu(zu)
    v = jax.nn.gelu(zv)
    vc = v - jnp.mean(v, axis=-1, keepdims=True)
    vn = vc * lax.rsqrt(jnp.mean(vc * vc, axis=-1, keepdims=True) + EPS) * gv_ref[...]
    vb = vn.astype(BF16)

    nh = dg // GMLP_CHUNK
    t_idx = lax.broadcasted_iota(jnp.int32, (GMLP_CHUNK, GMLP_CHUNK), 0)
    s_idx = lax.broadcasted_iota(jnp.int32, (GMLP_CHUNK, GMLP_CHUNK), 1)
    causal = t_idx >= s_idx
    wm = [jnp.where(causal, ws_ref[hd], 0.0).astype(BF16) for hd in range(nh)]
    rows = []
    for c in range(tm // GMLP_CHUNK):
        r0 = c * GMLP_CHUNK
        cols = []
        for hd in range(nh):
            c0 = hd * GMLP_CHUNK
            s = jnp.dot(wm[hd], vb[r0:r0 + GMLP_CHUNK, c0:c0 + GMLP_CHUNK], preferred_element_type=F32)
            cols.append(s + bs_ref[hd])
        rows.append(jnp.concatenate(cols, axis=1))
    yg = u * jnp.concatenate(rows, axis=0)
    o_ref[...] = _rms(yg, gout_ref[...]).astype(BF16)


def _gmlp(h, gmix, win, gv, ws, bs_full, gout, *, tm):
    m, d = h.shape
    dg = gv.shape[0]
    ublk = (win.shape[1] - 2 * dg) // dg
    kern = functools.partial(_gmlp_kernel, tm=tm, dg=dg)
    return pl.pallas_call(
        kern,
        grid=(m // tm,),
        in_specs=[
            pl.BlockSpec((tm, d), lambda i: (i, 0)),
            _resident((1, d)),
            pl.BlockSpec((d, dg), lambda i: (0, ublk), pipeline_mode=pl.Buffered(1)),
            pl.BlockSpec((d, dg), lambda i: (0, ublk + 1), pipeline_mode=pl.Buffered(1)),
            _resident((1, dg)),
            _resident(ws.shape),
            _resident(bs_full.shape),
            _resident((1, dg)),
        ],
        out_specs=pl.BlockSpec((tm, dg), lambda i: (i, 0)),
        out_shape=jax.ShapeDtypeStruct((m, dg), BF16),
        compiler_params=pltpu.CompilerParams(
            dimension_semantics=("parallel",),
            vmem_limit_bytes=48 << 20,
        ),
        name="gmlp",
    )(h, gmix.reshape(1, d), win, win, gv.reshape(1, dg), ws, bs_full, gout.reshape(1, dg))


def _outproj_kernel(h_ref, ys_ref, yg_ref, wo_ref, o_ref, *, ds):
    acc = jnp.dot(ys_ref[...], wo_ref[:ds, :], preferred_element_type=F32)
    acc += jnp.dot(yg_ref[...], wo_ref[ds:, :], preferred_element_type=F32)
    o_ref[...] = h_ref[...] + acc


def _outproj(h, ys, yg, wo, *, tm):
    m, d = h.shape
    ds = ys.shape[1]
    dg = yg.shape[1]
    kern = functools.partial(_outproj_kernel, ds=ds)
    return pl.pallas_call(
        kern,
        grid=(m // tm,),
        in_specs=[
            pl.BlockSpec((tm, d), lambda i: (i, 0)),
            pl.BlockSpec((tm, ds), lambda i: (i, 0)),
            pl.BlockSpec((tm, dg), lambda i: (i, 0)),
            _resident(wo.shape),
        ],
        out_specs=pl.BlockSpec((tm, d), lambda i: (i, 0)),
        out_shape=jax.ShapeDtypeStruct((m, d), F32),
        compiler_params=pltpu.CompilerParams(
            dimension_semantics=("parallel",),
            vmem_limit_bytes=48 << 20,
        ),
        name="outproj",
    )(h, ys, yg, wo)


def _ple_kernel(h_ref, p_ref, gple_ref, wg_ref, wp_ref, gfin_ref, o_ref):
    h = h_ref[...]
    hn = _rms(h, gple_ref[...]).astype(BF16)
    gate = jax.nn.sigmoid(jnp.dot(hn, wg_ref[...], preferred_element_type=F32))
    pp = jnp.dot(p_ref[...].astype(BF16), wp_ref[...], preferred_element_type=F32)
    o_ref[...] = _rms(h + gate * pp, gfin_ref[...])


def _ple(h, p, gple, wg, wp, gfin, *, tm):
    m, d = h.shape
    dp = p.shape[1]
    return pl.pallas_call(
        _ple_kernel,
        grid=(m // tm,),
        in_specs=[
            pl.BlockSpec((tm, d), lambda i: (i, 0)),
            pl.BlockSpec((tm, dp), lambda i: (i, 0)),
            _resident((1, d)),
            _resident(wg.shape),
            _resident(wp.shape),
            _resident((1, d)),
        ],
        out_specs=pl.BlockSpec((tm, d), lambda i: (i, 0)),
        out_shape=jax.ShapeDtypeStruct((m, d), F32),
        compiler_params=pltpu.CompilerParams(
            dimension_semantics=("parallel",),
            vmem_limit_bytes=48 << 20,
        ),
        name="ple",
    )(h, p, gple.reshape(1, d), wg, wp, gfin.reshape(1, d))


def kernel(x, p, norm_ffn1, w1_gate, w1_up, w1_down, norm_mix, w_in, ssm_log_dt, ssm_a_re, ssm_a_im, ssm_b_re, ssm_b_im, ssm_c_re, ssm_c_im, ssm_d, ssm_w_glu, gmlp_norm_v, gmlp_w_s, gmlp_b_s, norm_ssm_out, norm_gmlp_out, w_out, norm_ffn2, w2_gate, w2_up, w2_down, norm_ple, w_ple_gate, w_ple_proj, norm_final):
    bsz, seqlen, d = x.shape
    depth = p.shape[0]
    m = bsz * seqlen
    dssm = ssm_d.shape[1]
    assert depth == 1, "the per-layer embedding kernel fuses the final norm"
    h = x.reshape(m, d)
    for i in range(depth):
        nrow, nf = m // FFN_TM, w2_gate.shape[2] // FFN_TF
        jobs = [_grid_tiles(w2_gate[i], nrow, nf), _grid_tiles(w2_up[i], nrow, nf),
                _grid_tiles(w2_down[i], nrow, nf, swap=True),
                _row_tiles(w_in[i], nrow), _row_tiles(ssm_w_glu[i], nrow)]
        h, (w2g, w2u, w2d, w_in_b, wglu_b) = _ffn(
            h, norm_ffn1[i], w1_gate[i].astype(BF16), w1_up[i].astype(BF16), w1_down[i].astype(BF16),
            jobs, tm=FFN_TM, tf=FFN_TF)

        abr, abi, bbr, bbi, cimn = _s5_prep(ssm_log_dt[i], ssm_a_re[i], ssm_a_im[i],
                                            ssm_b_re[i], ssm_b_im[i], ssm_c_im[i])
        bblk = jnp.concatenate([_block_diag_in(bbr), _block_diag_in(bbi)], axis=2).astype(BF16)
        cblk = jnp.concatenate([_block_diag_out(ssm_c_re[i]), _block_diag_out(cimn)], axis=1).astype(BF16)
        nblk = bblk.shape[0]
        ab_shape = (nblk, V7X_SUBLANES, V7X_LANES)
        nsteps = bsz * (seqlen // S5_TC) + 1
        jobs = [_step_tiles(w_out[i], nsteps), _step_tiles(w_ple_gate[i], nsteps),
                _step_tiles(w_ple_proj[i], nsteps)]
        ys, (w_out_b, wpg_b, wpp_b) = _s5(
            h.reshape(bsz, seqlen, d), norm_mix[i], w_in_b, bblk, cblk,
            abr.reshape(ab_shape), abi.reshape(ab_shape), ssm_d[i], wglu_b, norm_ssm_out[i], jobs, tc=S5_TC)

        nh, ck = gmlp_b_s.shape[1:]
        bs_full = jnp.broadcast_to(gmlp_b_s[i][:, :, None], (nh, ck, ck))
        yg = _gmlp(h, norm_mix[i], w_in_b, gmlp_norm_v[i], gmlp_w_s[i], bs_full, norm_gmlp_out[i], tm=ROW_TM)

        h = _outproj(h, ys.reshape(m, dssm), yg, w_out_b, tm=ROW_TM)

        h, _ = _ffn(h, norm_ffn2[i], w2g, w2u, w2d, tm=FFN_TM, tf=FFN_TF)

        h = _ple(h, p[i].reshape(m, -1), norm_ple[i], wpg_b, wpp_b, norm_final, tm=ROW_TM)
    return h.reshape(bsz, seqlen, d)
```

```python
import functools
from typing import Callable, NamedTuple

import jax
import jax.numpy as jnp
from jax import lax
from jax.experimental import pallas as pl
from jax.experimental.pallas import tpu as pltpu

F32 = jnp.float32
BF16 = jnp.bfloat16
EPS = 1e-6

V7X_LANES = 128
V7X_SUBLANES = 8
V7X_VMEM_BYTES = 64 * 1024 * 1024

SSM_GROUP = 16
SSM_STATE = 64
SSM_BLOCK_GROUPS = 16
GMLP_CHUNK = 128
SCAN_ROW_PITCH = 12

FFN_TM = 1024
FFN_TF = 512
S5_TC = 256
ROW_TM = 512


def _rms(x, g):
    ms = jnp.mean(x * x, axis=-1, keepdims=True)
    return x * lax.rsqrt(ms + EPS) * g


def _resident(shape):
    n = len(shape)
    return pl.BlockSpec(shape, lambda *_: (0,) * n, pipeline_mode=pl.Buffered(1))


class _CastJob(NamedTuple):
    src: jax.Array
    block: tuple
    index_map: Callable


def _cast_specs(jobs):
    specs = [pl.BlockSpec(j.block, j.index_map) for j in jobs]
    shapes = [jax.ShapeDtypeStruct(j.src.shape, BF16) for j in jobs]
    return specs, shapes


def _run_casts(srcs, dsts):
    for src, dst in zip(srcs, dsts):
        dst[...] = src[...].astype(BF16)


BF16_ROWS = 2 * V7X_SUBLANES


def _slab_job(w, nsteps, step_of=lambda s: s):
    r, c = w.shape
    rows = BF16_ROWS
    while r // rows > nsteps:
        rows *= 2
    n = r // rows
    assert r % rows == 0
    return _CastJob(w, (rows, c), lambda *g: (jnp.minimum(step_of(*g), n - 1), 0))


def _ffn_kernel(*refs, n_cast):
    n = n_cast
    x_ref, g_ref, wg_ref, wu_ref, wd_ref = refs[:5]
    cast_src, o_ref, cast_dst, xn_ref = refs[5:5 + n], refs[5 + n], refs[6 + n:6 + 2 * n], refs[6 + 2 * n]

    @pl.when(pl.program_id(1) == 0)
    def _():
        x = x_ref[...]
        xn_ref[...] = _rms(x, g_ref[...]).astype(BF16)
        o_ref[...] = x

    _run_casts(cast_src, cast_dst)
    xn = xn_ref[...]
    gate = jnp.dot(xn, wg_ref[...], preferred_element_type=F32)
    up = jnp.dot(xn, wu_ref[...], preferred_element_type=F32)
    act = (gate * jax.nn.sigmoid(gate) * (0.5 * up)).astype(BF16)
    o_ref[...] += jnp.dot(act, wd_ref[...], preferred_element_type=F32)


def _ffn(x, g, wg, wu, wd, casts=(), *, tm, tf):
    m, d = x.shape
    f = wg.shape[1]
    row = pl.BlockSpec((tm, d), lambda i, j: (i, 0))
    cast_specs, cast_shapes = _cast_specs(casts)
    vmem = (2 * tm * d * 4) * 2 + tm * d * 2 + 3 * 2 * d * tf * 2 + 4 * tm * tf * 4
    vmem += sum(c.block[0] * c.block[1] * 2 * (4 + 2) for c in casts)
    outs = pl.pallas_call(
        functools.partial(_ffn_kernel, n_cast=len(casts)),
        grid=(m // tm, f // tf),
        in_specs=[
            row,
            pl.BlockSpec((1, d), lambda i, j: (0, 0)),
            pl.BlockSpec((d, tf), lambda i, j: (0, j)),
            pl.BlockSpec((d, tf), lambda i, j: (0, j)),
            pl.BlockSpec((tf, d), lambda i, j: (j, 0)),
        ] + cast_specs,
        out_specs=[row] + cast_specs,
        out_shape=[jax.ShapeDtypeStruct((m, d), F32)] + cast_shapes,
        scratch_shapes=[pltpu.VMEM((tm, d), BF16)],
        compiler_params=pltpu.CompilerParams(
            dimension_semantics=("parallel", "arbitrary"),
            vmem_limit_bytes=min(vmem + (8 << 20), V7X_VMEM_BYTES - (2 << 20)),
        ),
        name="ffn",
    )(x, g.reshape(1, d), wg, wu, wd, *[c.src for c in casts])
    return outs[0], list(outs[1:])


def _zoh(logdt, a_re, a_im):
    dt = jnp.exp(logdt)
    lr = jnp.minimum(a_re, -1e-4)
    li = a_im
    mag = jnp.exp(lr * dt)
    ang = li * dt
    abr = mag * jnp.cos(ang)
    abi = mag * jnp.sin(ang)
    den = lr * lr + li * li
    xr = abr - 1.0
    xi = abi
    zr = (xr * lr + xi * li) / den
    zi = (xi * lr - xr * li) / den
    return abr, abi, zr, zi


def _s5_prep_kernel(ldt_ref, are_ref, aim_ref, ldtb_ref, areb_ref, aimb_ref, bre_ref, bim_ref, cim_ref,
                    abr_ref, abi_ref, bbr_ref, bbi_ref, cimn_ref):
    abr, abi, _, _ = _zoh(ldt_ref[...], are_ref[...], aim_ref[...])
    abr_ref[...] = abr
    abi_ref[...] = abi
    _, _, zr, zi = _zoh(ldtb_ref[...], areb_ref[...], aimb_ref[...])
    br = bre_ref[...]
    bi = bim_ref[...]
    bbr_ref[...] = zr * br - zi * bi
    bbi_ref[...] = zr * bi + zi * br
    cimn_ref[...] = -cim_ref[...]


def _s5_prep(log_dt, a_re, a_im, b_re, b_im, c_im):
    g, n = a_re.shape
    p = b_re.shape[2]
    small = (g * n // V7X_LANES, V7X_LANES)
    big = (g * p * n // V7X_LANES, V7X_LANES)

    def bc(a):
        return jnp.broadcast_to(a[:, None, :], (g, p, n)).reshape(big)

    ldt = jnp.broadcast_to(log_dt[:, None], (g, n))
    outs = pl.pallas_call(
        _s5_prep_kernel,
        out_shape=[jax.ShapeDtypeStruct(small, F32)] * 2 + [jax.ShapeDtypeStruct(big, F32)] * 3,
        name="s5_prep",
    )(ldt.reshape(small), a_re.reshape(small), a_im.reshape(small),
      bc(ldt), bc(a_re), bc(a_im),
      jnp.swapaxes(b_re, 1, 2).reshape(big), jnp.swapaxes(b_im, 1, 2).reshape(big), c_im.reshape(big))
    abr, abi, bbr, bbi, cimn = outs
    return (abr.reshape(g, n), abi.reshape(g, n),
            bbr.reshape(g, p, n), bbi.reshape(g, p, n), cimn.reshape(g, p, n))


def _block_diag_in(w):
    g, p, n = w.shape
    nb = g // SSM_BLOCK_GROUPS
    w4 = w.reshape(nb, SSM_BLOCK_GROUPS, p, n)
    eye = jnp.eye(SSM_BLOCK_GROUPS, dtype=w.dtype)
    return jnp.einsum("kipn,ij->kipjn", w4, eye).reshape(nb, SSM_BLOCK_GROUPS * p, SSM_BLOCK_GROUPS * n)


def _block_diag_out(w):
    g, p, n = w.shape
    nb = g // SSM_BLOCK_GROUPS
    w4 = w.reshape(nb, SSM_BLOCK_GROUPS, p, n)
    eye = jnp.eye(SSM_BLOCK_GROUPS, dtype=w.dtype)
    return jnp.einsum("kipn,ij->kinjp", w4, eye).reshape(nb, SSM_BLOCK_GROUPS * n, SSM_BLOCK_GROUPS * p)


def _s5_kernel(*refs, tc, nblk, bch, bst, chunks_per_seq, n_cast):
    h_ref, gmix_ref, win_ref, bblk_ref, cblk_ref, abr_ref, abi_ref, d_ref, wglu_ref, gout_ref = refs[:10]
    cast_src, o_ref, cast_dst = refs[10:10 + n_cast], refs[10 + n_cast], refs[11 + n_cast:11 + 2 * n_cast]
    dre0_ref, dim0_ref, u0_ref, dre1_ref, dim1_ref, u1_ref, st_ref = refs[11 + 2 * n_cast:]
    ntile = bst // V7X_LANES
    nb = tc // V7X_SUBLANES
    step_id = pl.program_id(0)
    bufs = ((dre0_ref, dim0_ref, u0_ref), (dre1_ref, dim1_ref, u1_ref))
    _run_casts(cast_src, cast_dst)

    @pl.when(step_id == 0)
    def _():
        st_ref[...] = jnp.zeros_like(st_ref)
        for ref in bufs[1]:
            ref[...] = jnp.zeros_like(ref)

    def step(wr, rd):
        dre_w, dim_w, u_w = wr
        dre_r, dim_r, u_r = rd

        keep = jnp.where((step_id - 1) % chunks_per_seq == 0, 0.0, 1.0).astype(F32)
        ar = [abr_ref[k] for k in range(nblk)]
        ai = [abi_ref[k] for k in range(nblk)]
        s = [st_ref[i] * keep for i in range(2 * nblk)]
        for tb in range(nb):
            for r in range(V7X_SUBLANES):
                rows = slice(r * SCAN_ROW_PITCH, r * SCAN_ROW_PITCH + ntile)
                for k in range(nblk):
                    sre, sim = s[2 * k], s[2 * k + 1]
                    nre = ar[k] * sre - ai[k] * sim + dre_r[k, tb, rows, :]
                    nim = ar[k] * sim + ai[k] * sre + dim_r[k, tb, rows, :]
                    dre_r[k, tb, rows, :] = nre
                    dim_r[k, tb, rows, :] = nim
                    s[2 * k], s[2 * k + 1] = nre, nim
        for i in range(2 * nblk):
            st_ref[i] = s[i]

        hn = _rms(h_ref[0], gmix_ref[...]).astype(BF16)
        u = jnp.dot(hn, win_ref[...], preferred_element_type=F32)
        u_w[...] = u
        ub = u.astype(BF16)
        for k in range(nblk):
            drv = jnp.dot(ub[:, k * bch:(k + 1) * bch], bblk_ref[k], preferred_element_type=F32)
            for j in range(ntile):
                lo = j * V7X_LANES
                tile = pl.ds(j, V7X_SUBLANES, stride=SCAN_ROW_PITCH)
                dre_w[k, :, tile, :] = drv[:, lo:lo + V7X_LANES].reshape(nb, V7X_SUBLANES, V7X_LANES)
                dim_w[k, :, tile, :] = drv[:, bst + lo:bst + lo + V7X_LANES].reshape(nb, V7X_SUBLANES, V7X_LANES)

        ys = []
        for k in range(nblk):
            parts = []
            for ref in (dre_r, dim_r):
                for j in range(ntile):
                    tile = pl.ds(j, V7X_SUBLANES, stride=SCAN_ROW_PITCH)
                    parts.append(ref[k, :, tile, :].reshape(tc, V7X_LANES))
            lhs = jnp.concatenate(parts, axis=1).astype(BF16)
            ys.append(jnp.dot(lhs, cblk_ref[k], preferred_element_type=F32))
        y = jnp.concatenate(ys, axis=1) + d_ref[...] * u_r[...]
        y = jax.nn.gelu(y)
        y = y * jax.nn.sigmoid(jnp.dot(y.astype(BF16), wglu_ref[...], preferred_element_type=F32))
        o_ref[0] = _rms(y, gout_ref[...]).astype(BF16)

    @pl.when(step_id % 2 == 0)
    def _():
        step(bufs[0], bufs[1])

    @pl.when(step_id % 2 == 1)
    def _():
        step(bufs[1], bufs[0])


def _s5(h3, gmix, win, bblk, cblk, abr, abi, dskip, wglu, gout, casts=(), *, tc):
    b, l, d = h3.shape
    dssm = dskip.shape[0]
    nblk, bch, bst2 = bblk.shape
    bst = bst2 // 2
    ntile = bst // V7X_LANES
    assert ntile == V7X_SUBLANES and tc % V7X_SUBLANES == 0
    nb = tc // V7X_SUBLANES
    nch = l // tc
    last = b * nch - 1
    kern = functools.partial(_s5_kernel, tc=tc, nblk=nblk, bch=bch, bst=bst, chunks_per_seq=nch,
                             n_cast=len(casts))
    cast_specs, cast_shapes = _cast_specs(casts)
    dbuf = pltpu.VMEM((nblk, nb, V7X_SUBLANES * SCAN_ROW_PITCH, V7X_LANES), F32)
    ubuf = pltpu.VMEM((tc, dssm), F32)

    def in_chunk(s):
        c = jnp.minimum(s, last)
        return (c // nch, c % nch, 0)

    def out_chunk(s):
        c = jnp.maximum(s - 1, 0)
        return (c // nch, c % nch, 0)

    outs = pl.pallas_call(
        kern,
        grid=(b * nch + 1,),
        in_specs=[
            pl.BlockSpec((1, tc, d), in_chunk),
            _resident((1, d)),
            pl.BlockSpec((d, dssm), lambda s: (0, 0), pipeline_mode=pl.Buffered(1)),
            _resident(bblk.shape),
            _resident(cblk.shape),
            _resident(abr.shape),
            _resident(abi.shape),
            _resident((1, dssm)),
            _resident(wglu.shape),
            _resident((1, dssm)),
        ] + cast_specs,
        out_specs=[pl.BlockSpec((1, tc, dssm), out_chunk)] + cast_specs,
        out_shape=[jax.ShapeDtypeStruct((b, l, dssm), BF16)] + cast_shapes,
        scratch_shapes=[
            dbuf, dbuf, ubuf, dbuf, dbuf, ubuf,
            pltpu.VMEM((2 * nblk, V7X_SUBLANES, V7X_LANES), F32),
        ],
        compiler_params=pltpu.CompilerParams(
            dimension_semantics=("arbitrary",),
            vmem_limit_bytes=56 << 20,
        ),
        name="s5",
    )(h3, gmix.reshape(1, d), win, bblk, cblk, abr, abi, dskip.reshape(1, dssm), wglu, gout.reshape(1, dssm),
      *[c.src for c in casts])
    return outs[0], list(outs[1:])


def _gmlp_kernel(h_ref, gmix_ref, wu_ref, wv_ref, gv_ref, ws_ref, bs_ref, gout_ref, o_ref, *, tm, dg):
    hn = _rms(h_ref[...], gmix_ref[...]).astype(BF16)
    zu = jnp.dot(hn, wu_ref[...], preferred_element_type=F32)
    zv = jnp.dot(hn, wv_ref[...], preferred_element_type=F32)
    u = jax.nn.gelu(zu)
    v = jax.nn.gelu(zv)
    vc = v - jnp.mean(v, axis=-1, keepdims=True)
    vn = vc * lax.rsqrt(jnp.mean(vc * vc, axis=-1, keepdims=True) + EPS) * gv_ref[...]
    vb = vn.astype(BF16)

    nh = dg // GMLP_CHUNK
    t_idx = lax.broadcasted_iota(jnp.int32, (GMLP_CHUNK, GMLP_CHUNK), 0)
    s_idx = lax.broadcasted_iota(jnp.int32, (GMLP_CHUNK, GMLP_CHUNK), 1)
    causal = t_idx >= s_idx
    wm = [jnp.where(causal, ws_ref[hd], 0.0).astype(BF16) for hd in range(nh)]
    rows = []
    for c in range(tm // GMLP_CHUNK):
        r0 = c * GMLP_CHUNK
        cols = []
        for hd in range(nh):
            c0 = hd * GMLP_CHUNK
            s = jnp.dot(wm[hd], vb[r0:r0 + GMLP_CHUNK, c0:c0 + GMLP_CHUNK], preferred_element_type=F32)
            cols.append(s + bs_ref[hd])
        rows.append(jnp.concatenate(cols, axis=1))
    yg = u * jnp.concatenate(rows, axis=0)
    o_ref[...] = _rms(yg, gout_ref[...]).astype(BF16)


def _gmlp(h, gmix, win, gv, ws, bs_full, gout, *, tm):
    m, d = h.shape
    dg = gv.shape[0]
    ublk = (win.shape[1] - 2 * dg) // dg
    kern = functools.partial(_gmlp_kernel, tm=tm, dg=dg)
    return pl.pallas_call(
        kern,
        grid=(m // tm,),
        in_specs=[
            pl.BlockSpec((tm, d), lambda i: (i, 0)),
            _resident((1, d)),
            pl.BlockSpec((d, dg), lambda i: (0, ublk), pipeline_mode=pl.Buffered(1)),
            pl.BlockSpec((d, dg), lambda i: (0, ublk + 1), pipeline_mode=pl.Buffered(1)),
            _resident((1, dg)),
            _resident(ws.shape),
            _resident(bs_full.shape),
            _resident((1, dg)),
        ],
        out_specs=pl.BlockSpec((tm, dg), lambda i: (i, 0)),
        out_shape=jax.ShapeDtypeStruct((m, dg), BF16),
        compiler_params=pltpu.CompilerParams(
            dimension_semantics=("parallel",),
            vmem_limit_bytes=48 << 20,
        ),
        name="gmlp",
    )(h, gmix.reshape(1, d), win, win, gv.reshape(1, dg), ws, bs_full, gout.reshape(1, dg))


def _outproj_kernel(h_ref, ys_ref, yg_ref, wo_ref, o_ref, *, ds):
    acc = jnp.dot(ys_ref[...], wo_ref[:ds, :], preferred_element_type=F32)
    acc += jnp.dot(yg_ref[...], wo_ref[ds:, :], preferred_element_type=F32)
    o_ref[...] = h_ref[...] + acc


def _outproj(h, ys, yg, wo, *, tm):
    m, d = h.shape
    ds = ys.shape[1]
    dg = yg.shape[1]
    kern = functools.partial(_outproj_kernel, ds=ds)
    return pl.pallas_call(
        kern,
        grid=(m // tm,),
        in_specs=[
            pl.BlockSpec((tm, d), lambda i: (i, 0)),
            pl.BlockSpec((tm, ds), lambda i: (i, 0)),
            pl.BlockSpec((tm, dg), lambda i: (i, 0)),
            _resident(wo.shape),
        ],
        out_specs=pl.BlockSpec((tm, d), lambda i: (i, 0)),
        out_shape=jax.ShapeDtypeStruct((m, d), F32),
        compiler_params=pltpu.CompilerParams(
            dimension_semantics=("parallel",),
            vmem_limit_bytes=48 << 20,
        ),
        name="outproj",
    )(h, ys, yg, wo)


def _ple_kernel(h_ref, p_ref, gple_ref, wg_ref, wp_ref, gfin_ref, o_ref):
    h = h_ref[...]
    hn = _rms(h, gple_ref[...]).astype(BF16)
    gate = jax.nn.sigmoid(jnp.dot(hn, wg_ref[...], preferred_element_type=F32))
    pp = jnp.dot(p_ref[...].astype(BF16), wp_ref[...], preferred_element_type=F32)
    o_ref[...] = _rms(h + gate * pp, gfin_ref[...])


def _ple(h, p, gple, wg, wp, gfin, *, tm):
    m, d = h.shape
    dp = p.shape[1]
    return pl.pallas_call(
        _ple_kernel,
        grid=(m // tm,),
        in_specs=[
            pl.BlockSpec((tm, d), lambda i: (i, 0)),
            pl.BlockSpec((tm, dp), lambda i: (i, 0)),
            _resident((1, d)),
            _resident(wg.shape),
            _resident(wp.shape),
            _resident((1, d)),
        ],
        out_specs=pl.BlockSpec((tm, d), lambda i: (i, 0)),
        out_shape=jax.ShapeDtypeStruct((m, d), F32),
        compiler_params=pltpu.CompilerParams(
            dimension_semantics=("parallel",),
            vmem_limit_bytes=48 << 20,
        ),
        name="ple",
    )(h, p, gple.reshape(1, d), wg, wp, gfin.reshape(1, d))


def kernel(x, p, norm_ffn1, w1_gate, w1_up, w1_down, norm_mix, w_in, ssm_log_dt, ssm_a_re, ssm_a_im, ssm_b_re, ssm_b_im, ssm_c_re, ssm_c_im, ssm_d, ssm_w_glu, gmlp_norm_v, gmlp_w_s, gmlp_b_s, norm_ssm_out, norm_gmlp_out, w_out, norm_ffn2, w2_gate, w2_up, w2_down, norm_ple, w_ple_gate, w_ple_proj, norm_final):
    bsz, seqlen, d = x.shape
    depth = p.shape[0]
    m = bsz * seqlen
    dssm = ssm_d.shape[1]
    assert depth == 1, "the per-layer embedding kernel fuses the final norm"
    h = x.reshape(m, d)
    for i in range(depth):
        nrow, nf = m // FFN_TM, w2_gate.shape[2] // FFN_TF
        jobs = [_slab_job(w, nrow * nf, lambda r, c: r * nf + c)
                for w in (w2_gate[i], w2_up[i], w2_down[i], w_in[i], ssm_w_glu[i])]
        h, (w2g, w2u, w2d, w_in_b, wglu_b) = _ffn(
            h, norm_ffn1[i], w1_gate[i].astype(BF16), w1_up[i].astype(BF16), w1_down[i].astype(BF16),
            jobs, tm=FFN_TM, tf=FFN_TF)

        abr, abi, bbr, bbi, cimn = _s5_prep(ssm_log_dt[i], ssm_a_re[i], ssm_a_im[i],
                                            ssm_b_re[i], ssm_b_im[i], ssm_c_im[i])
        bblk = jnp.concatenate([_block_diag_in(bbr), _block_diag_in(bbi)], axis=2).astype(BF16)
        cblk = jnp.concatenate([_block_diag_out(ssm_c_re[i]), _block_diag_out(cimn)], axis=1).astype(BF16)
        nblk = bblk.shape[0]
        ab_shape = (nblk, V7X_SUBLANES, V7X_LANES)
        nsteps = bsz * (seqlen // S5_TC) + 1
        jobs = [_slab_job(w, nsteps) for w in (w_out[i], w_ple_gate[i], w_ple_proj[i])]
        ys, (w_out_b, wpg_b, wpp_b) = _s5(
            h.reshape(bsz, seqlen, d), norm_mix[i], w_in_b, bblk, cblk,
            abr.reshape(ab_shape), abi.reshape(ab_shape), ssm_d[i], wglu_b, norm_ssm_out[i], jobs, tc=S5_TC)

        nh, ck = gmlp_b_s.shape[1:]
        bs_full = jnp.broadcast_to(gmlp_b_s[i][:, :, None], (nh, ck, ck))
        yg = _gmlp(h, norm_mix[i], w_in_b, gmlp_norm_v[i], gmlp_w_s[i], bs_full, norm_gmlp_out[i], tm=ROW_TM)

        h = _outproj(h, ys.reshape(m, dssm), yg, w_out_b, tm=ROW_TM)

        h, _ = _ffn(h, norm_ffn2[i], w2g, w2u, w2d, tm=FFN_TM, tf=FFN_TF)

        h = _ple(h, p[i].reshape(m, -1), norm_ple[i], wpg_b, wpp_b, norm_final, tm=ROW_TM)
    return h.reshape(bsz, seqlen, d)
```

```python
import functools
from typing import Callable, NamedTuple

import jax
import jax.numpy as jnp
from jax import lax
from jax.experimental import pallas as pl
from jax.experimental.pallas import tpu as pltpu

F32 = jnp.float32
BF16 = jnp.bfloat16
EPS = 1e-6

V7X_LANES = 128
V7X_SUBLANES = 8
V7X_VMEM_BYTES = 64 * 1024 * 1024

SSM_GROUP = 16
SSM_STATE = 64
SSM_BLOCK_GROUPS = 16
GMLP_CHUNK = 128
SCAN_ROW_PITCH = 12

FFN_TM = 1024
FFN_TF = 512
S5_TC = 256
ROW_TM = 512


def _rms(x, g):
    ms = jnp.mean(x * x, axis=-1, keepdims=True)
    return x * lax.rsqrt(ms + EPS) * g


def _resident(shape):
    n = len(shape)
    return pl.BlockSpec(shape, lambda *_: (0,) * n, pipeline_mode=pl.Buffered(1))


class _CastJob(NamedTuple):
    src: jax.Array
    block: tuple
    index_map: Callable


def _cast_specs(jobs):
    specs = [pl.BlockSpec(j.block, j.index_map) for j in jobs]
    shapes = [jax.ShapeDtypeStruct(j.src.shape, BF16) for j in jobs]
    return specs, shapes


def _run_casts(srcs, dsts):
    for src, dst in zip(srcs, dsts):
        dst[...] = src[...].astype(BF16)


BF16_ROWS = 2 * V7X_SUBLANES


def _slab_job(w, nsteps, step_of=lambda s: s):
    r, c = w.shape
    rows = BF16_ROWS
    while r // rows > nsteps:
        rows *= 2
    n = r // rows
    assert r % rows == 0
    return _CastJob(w, (rows, c), lambda *g: (jnp.minimum(step_of(*g), n - 1), 0))


def _ffn_kernel(*refs, n_cast):
    n = n_cast
    x_ref, g_ref, wg_ref, wu_ref, wd_ref = refs[:5]
    cast_src, o_ref, cast_dst, xn_ref = refs[5:5 + n], refs[5 + n], refs[6 + n:6 + 2 * n], refs[6 + 2 * n]

    @pl.when(pl.program_id(1) == 0)
    def _():
        x = x_ref[...]
        xn_ref[...] = _rms(x, g_ref[...]).astype(BF16)
        o_ref[...] = x

    _run_casts(cast_src, cast_dst)
    xn = xn_ref[...]
    gate = jnp.dot(xn, wg_ref[...], preferred_element_type=F32)
    up = jnp.dot(xn, wu_ref[...], preferred_element_type=F32)
    act = (gate * jax.nn.sigmoid(gate) * (0.5 * up)).astype(BF16)
    o_ref[...] += jnp.dot(act, wd_ref[...], preferred_element_type=F32)


def _ffn(x, g, wg, wu, wd, casts=(), *, tm, tf):
    m, d = x.shape
    f = wg.shape[1]
    row = pl.BlockSpec((tm, d), lambda i, j: (i, 0))
    cast_specs, cast_shapes = _cast_specs(casts)
    vmem = (2 * tm * d * 4) * 2 + tm * d * 2 + 3 * 2 * d * tf * 2 + 4 * tm * tf * 4
    vmem += sum(c.block[0] * c.block[1] * 2 * (4 + 2) for c in casts)
    outs = pl.pallas_call(
        functools.partial(_ffn_kernel, n_cast=len(casts)),
        grid=(m // tm, f // tf),
        in_specs=[
            row,
            pl.BlockSpec((1, d), lambda i, j: (0, 0)),
            pl.BlockSpec((d, tf), lambda i, j: (0, j)),
            pl.BlockSpec((d, tf), lambda i, j: (0, j)),
            pl.BlockSpec((tf, d), lambda i, j: (j, 0)),
        ] + cast_specs,
        out_specs=[row] + cast_specs,
        out_shape=[jax.ShapeDtypeStruct((m, d), F32)] + cast_shapes,
        scratch_shapes=[pltpu.VMEM((tm, d), BF16)],
        compiler_params=pltpu.CompilerParams(
            dimension_semantics=("parallel", "arbitrary"),
            vmem_limit_bytes=min(vmem + (8 << 20), V7X_VMEM_BYTES - (2 << 20)),
        ),
        name="ffn",
    )(x, g.reshape(1, d), wg, wu, wd, *[c.src for c in casts])
    return outs[0], list(outs[1:])


def _ffn_stream_kernel(x_ref, g_ref, wg_hbm, wu_hbm, wd_hbm, o_ref, xn_ref, wg_buf, wu_buf, wd_buf, sem,
                       *, nf, tf):
    def copies(f, slot):
        lo = pl.multiple_of(f * tf, tf)
        return (pltpu.make_async_copy(wg_hbm.at[:, pl.ds(lo, tf)], wg_buf.at[slot], sem.at[0, slot]),
                pltpu.make_async_copy(wu_hbm.at[:, pl.ds(lo, tf)], wu_buf.at[slot], sem.at[1, slot]),
                pltpu.make_async_copy(wd_hbm.at[pl.ds(lo, tf), :], wd_buf.at[slot], sem.at[2, slot]))

    def start(f, slot):
        for c in copies(f, slot):
            c.start()

    def finish(f, slot):
        for c in copies(f, slot):
            c.wait()
        xn = xn_ref[...]
        gate = jnp.dot(xn, wg_buf[slot], preferred_element_type=F32)
        up = jnp.dot(xn, wu_buf[slot], preferred_element_type=F32)
        act = (gate * jax.nn.sigmoid(gate) * (0.5 * up)).astype(BF16)
        o_ref[...] += jnp.dot(act, wd_buf[slot], preferred_element_type=F32)

    start(0, 0)
    x = x_ref[...]
    xn_ref[...] = _rms(x, g_ref[...]).astype(BF16)
    o_ref[...] = x

    def pair(k, carry):
        start(2 * k + 1, 1)
        finish(2 * k, 0)
        start(2 * k + 2, 0)
        finish(2 * k + 1, 1)
        return carry

    lax.fori_loop(0, nf // 2, pair, 0)
    finish(nf - 1, 0)


def _ffn_stream(x, g, wg, wu, wd, *, tm, tf):
    m, d = x.shape
    f = wg.shape[1]
    nf = f // tf
    assert f % tf == 0 and nf % 2 == 1, "the tile-pair loop expects an odd number of hidden tiles"
    row = pl.BlockSpec((tm, d), lambda i: (i, 0))
    hbm = pl.BlockSpec(memory_space=pl.ANY)
    vmem = (2 * tm * d * 4) * 2 + tm * d * 2 + 3 * 2 * d * tf * 2 + 4 * tm * tf * 4
    return pl.pallas_call(
        functools.partial(_ffn_stream_kernel, nf=nf, tf=tf),
        grid=(m // tm,),
        in_specs=[row, pl.BlockSpec((1, d), lambda i: (0, 0)), hbm, hbm, hbm],
        out_specs=row,
        out_shape=jax.ShapeDtypeStruct((m, d), F32),
        scratch_shapes=[
            pltpu.VMEM((tm, d), BF16),
            pltpu.VMEM((2, d, tf), BF16),
            pltpu.VMEM((2, d, tf), BF16),
            pltpu.VMEM((2, tf, d), BF16),
            pltpu.SemaphoreType.DMA((3, 2)),
        ],
        compiler_params=pltpu.CompilerParams(
            dimension_semantics=("arbitrary",),
            vmem_limit_bytes=min(vmem + (8 << 20), V7X_VMEM_BYTES - (2 << 20)),
        ),
        name="ffn_stream",
    )(x, g.reshape(1, d), wg, wu, wd)


def _zoh(logdt, a_re, a_im):
    dt = jnp.exp(logdt)
    lr = jnp.minimum(a_re, -1e-4)
    li = a_im
    mag = jnp.exp(lr * dt)
    ang = li * dt
    abr = mag * jnp.cos(ang)
    abi = mag * jnp.sin(ang)
    den = lr * lr + li * li
    xr = abr - 1.0
    xi = abi
    zr = (xr * lr + xi * li) / den
    zi = (xi * lr - xr * li) / den
    return abr, abi, zr, zi


def _s5_prep_kernel(ldt_ref, are_ref, aim_ref, ldtb_ref, areb_ref, aimb_ref, bre_ref, bim_ref, cim_ref,
                    abr_ref, abi_ref, bbr_ref, bbi_ref, cimn_ref):
    abr, abi, _, _ = _zoh(ldt_ref[...], are_ref[...], aim_ref[...])
    abr_ref[...] = abr
    abi_ref[...] = abi
    _, _, zr, zi = _zoh(ldtb_ref[...], areb_ref[...], aimb_ref[...])
    br = bre_ref[...]
    bi = bim_ref[...]
    bbr_ref[...] = zr * br - zi * bi
    bbi_ref[...] = zr * bi + zi * br
    cimn_ref[...] = -cim_ref[...]


def _s5_prep(log_dt, a_re, a_im, b_re, b_im, c_im):
    g, n = a_re.shape
    p = b_re.shape[2]
    small = (g * n // V7X_LANES, V7X_LANES)
    big = (g * p * n // V7X_LANES, V7X_LANES)

    def bc(a):
        return jnp.broadcast_to(a[:, None, :], (g, p, n)).reshape(big)

    ldt = jnp.broadcast_to(log_dt[:, None], (g, n))
    outs = pl.pallas_call(
        _s5_prep_kernel,
        out_shape=[jax.ShapeDtypeStruct(small, F32)] * 2 + [jax.ShapeDtypeStruct(big, F32)] * 3,
        name="s5_prep",
    )(ldt.reshape(small), a_re.reshape(small), a_im.reshape(small),
      bc(ldt), bc(a_re), bc(a_im),
      jnp.swapaxes(b_re, 1, 2).reshape(big), jnp.swapaxes(b_im, 1, 2).reshape(big), c_im.reshape(big))
    abr, abi, bbr, bbi, cimn = outs
    return (abr.reshape(g, n), abi.reshape(g, n),
            bbr.reshape(g, p, n), bbi.reshape(g, p, n), cimn.reshape(g, p, n))


def _block_diag_in(w):
    g, p, n = w.shape
    nb = g // SSM_BLOCK_GROUPS
    w4 = w.reshape(nb, SSM_BLOCK_GROUPS, p, n)
    eye = jnp.eye(SSM_BLOCK_GROUPS, dtype=w.dtype)
    return jnp.einsum("kipn,ij->kipjn", w4, eye).reshape(nb, SSM_BLOCK_GROUPS * p, SSM_BLOCK_GROUPS * n)


def _block_diag_out(w):
    g, p, n = w.shape
    nb = g // SSM_BLOCK_GROUPS
    w4 = w.reshape(nb, SSM_BLOCK_GROUPS, p, n)
    eye = jnp.eye(SSM_BLOCK_GROUPS, dtype=w.dtype)
    return jnp.einsum("kipn,ij->kinjp", w4, eye).reshape(nb, SSM_BLOCK_GROUPS * n, SSM_BLOCK_GROUPS * p)


def _s5_kernel(*refs, tc, nblk, bch, bst, chunks_per_seq, n_cast):
    h_ref, gmix_ref, win_ref, bblk_ref, cblk_ref, abr_ref, abi_ref, d_ref, wglu_ref, gout_ref = refs[:10]
    cast_src, o_ref, cast_dst = refs[10:10 + n_cast], refs[10 + n_cast], refs[11 + n_cast:11 + 2 * n_cast]
    dre0_ref, dim0_ref, u0_ref, dre1_ref, dim1_ref, u1_ref, st_ref = refs[11 + 2 * n_cast:]
    ntile = bst // V7X_LANES
    nb = tc // V7X_SUBLANES
    step_id = pl.program_id(0)
    bufs = ((dre0_ref, dim0_ref, u0_ref), (dre1_ref, dim1_ref, u1_ref))
    _run_casts(cast_src, cast_dst)

    @pl.when(step_id == 0)
    def _():
        st_ref[...] = jnp.zeros_like(st_ref)
        for ref in bufs[1]:
            ref[...] = jnp.zeros_like(ref)

    def step(wr, rd):
        dre_w, dim_w, u_w = wr
        dre_r, dim_r, u_r = rd

        keep = jnp.where((step_id - 1) % chunks_per_seq == 0, 0.0, 1.0).astype(F32)
        ar = [abr_ref[k] for k in range(nblk)]
        ai = [abi_ref[k] for k in range(nblk)]
        s = [st_ref[i] * keep for i in range(2 * nblk)]
        for tb in range(nb):
            for r in range(V7X_SUBLANES):
                rows = slice(r * SCAN_ROW_PITCH, r * SCAN_ROW_PITCH + ntile)
                for k in range(nblk):
                    sre, sim = s[2 * k], s[2 * k + 1]
                    nre = ar[k] * sre - ai[k] * sim + dre_r[k, tb, rows, :]
                    nim = ar[k] * sim + ai[k] * sre + dim_r[k, tb, rows, :]
                    dre_r[k, tb, rows, :] = nre
                    dim_r[k, tb, rows, :] = nim
                    s[2 * k], s[2 * k + 1] = nre, nim
        for i in range(2 * nblk):
            st_ref[i] = s[i]

        hn = _rms(h_ref[0], gmix_ref[...]).astype(BF16)
        u = jnp.dot(hn, win_ref[...], preferred_element_type=F32)
        u_w[...] = u
        ub = u.astype(BF16)
        for k in range(nblk):
            drv = jnp.dot(ub[:, k * bch:(k + 1) * bch], bblk_ref[k], preferred_element_type=F32)
            for j in range(ntile):
                lo = j * V7X_LANES
                tile = pl.ds(j, V7X_SUBLANES, stride=SCAN_ROW_PITCH)
                dre_w[k, :, tile, :] = drv[:, lo:lo + V7X_LANES].reshape(nb, V7X_SUBLANES, V7X_LANES)
                dim_w[k, :, tile, :] = drv[:, bst + lo:bst + lo + V7X_LANES].reshape(nb, V7X_SUBLANES, V7X_LANES)

        ys = []
        for k in range(nblk):
            parts = []
            for ref in (dre_r, dim_r):
                for j in range(ntile):
                    tile = pl.ds(j, V7X_SUBLANES, stride=SCAN_ROW_PITCH)
                    parts.append(ref[k, :, tile, :].reshape(tc, V7X_LANES))
            lhs = jnp.concatenate(parts, axis=1).astype(BF16)
            ys.append(jnp.dot(lhs, cblk_ref[k], preferred_element_type=F32))
        y = jnp.concatenate(ys, axis=1) + d_ref[...] * u_r[...]
        y = jax.nn.gelu(y)
        y = y * jax.nn.sigmoid(jnp.dot(y.astype(BF16), wglu_ref[...], preferred_element_type=F32))
        o_ref[0] = _rms(y, gout_ref[...]).astype(BF16)

    @pl.when(step_id % 2 == 0)
    def _():
        step(bufs[0], bufs[1])

    @pl.when(step_id % 2 == 1)
    def _():
        step(bufs[1], bufs[0])


def _s5(h3, gmix, win, bblk, cblk, abr, abi, dskip, wglu, gout, casts=(), *, tc):
    b, l, d = h3.shape
    dssm = dskip.shape[0]
    nblk, bch, bst2 = bblk.shape
    bst = bst2 // 2
    ntile = bst // V7X_LANES
    assert ntile == V7X_SUBLANES and tc % V7X_SUBLANES == 0
    nb = tc // V7X_SUBLANES
    nch = l // tc
    last = b * nch - 1
    kern = functools.partial(_s5_kernel, tc=tc, nblk=nblk, bch=bch, bst=bst, chunks_per_seq=nch,
                             n_cast=len(casts))
    cast_specs, cast_shapes = _cast_specs(casts)
    dbuf = pltpu.VMEM((nblk, nb, V7X_SUBLANES * SCAN_ROW_PITCH, V7X_LANES), F32)
    ubuf = pltpu.VMEM((tc, dssm), F32)

    def in_chunk(s):
        c = jnp.minimum(s, last)
        return (c // nch, c % nch, 0)

    def out_chunk(s):
        c = jnp.maximum(s - 1, 0)
        return (c // nch, c % nch, 0)

    outs = pl.pallas_call(
        kern,
        grid=(b * nch + 1,),
        in_specs=[
            pl.BlockSpec((1, tc, d), in_chunk),
            _resident((1, d)),
            pl.BlockSpec((d, dssm), lambda s: (0, 0), pipeline_mode=pl.Buffered(1)),
            _resident(bblk.shape),
            _resident(cblk.shape),
            _resident(abr.shape),
            _resident(abi.shape),
            _resident((1, dssm)),
            _resident(wglu.shape),
            _resident((1, dssm)),
        ] + cast_specs,
        out_specs=[pl.BlockSpec((1, tc, dssm), out_chunk)] + cast_specs,
        out_shape=[jax.ShapeDtypeStruct((b, l, dssm), BF16)] + cast_shapes,
        scratch_shapes=[
            dbuf, dbuf, ubuf, dbuf, dbuf, ubuf,
            pltpu.VMEM((2 * nblk, V7X_SUBLANES, V7X_LANES), F32),
        ],
        compiler_params=pltpu.CompilerParams(
            dimension_semantics=("arbitrary",),
            vmem_limit_bytes=56 << 20,
        ),
        name="s5",
    )(h3, gmix.reshape(1, d), win, bblk, cblk, abr, abi, dskip.reshape(1, dssm), wglu, gout.reshape(1, dssm),
      *[c.src for c in casts])
    return outs[0], list(outs[1:])


def _gmlp_kernel(h_ref, gmix_ref, wu_ref, wv_ref, gv_ref, ws_ref, bs_ref, gout_ref, o_ref, *, tm, dg):
    hn = _rms(h_ref[...], gmix_ref[...]).astype(BF16)
    zu = jnp.dot(hn, wu_ref[...], preferred_element_type=F32)
    zv = jnp.dot(hn, wv_ref[...], preferred_element_type=F32)
    u = jax.nn.gelu(zu)
    v = jax.nn.gelu(zv)
    vc = v - jnp.mean(v, axis=-1, keepdims=True)
    vn = vc * lax.rsqrt(jnp.mean(vc * vc, axis=-1, keepdims=True) + EPS) * gv_ref[...]
    vb = vn.astype(BF16)

    nh = dg // GMLP_CHUNK
    t_idx = lax.broadcasted_iota(jnp.int32, (GMLP_CHUNK, GMLP_CHUNK), 0)
    s_idx = lax.broadcasted_iota(jnp.int32, (GMLP_CHUNK, GMLP_CHUNK), 1)
    causal = t_idx >= s_idx
    wm = [jnp.where(causal, ws_ref[hd], 0.0).astype(BF16) for hd in range(nh)]
    rows = []
    for c in range(tm // GMLP_CHUNK):
        r0 = c * GMLP_CHUNK
        cols = []
        for hd in range(nh):
            c0 = hd * GMLP_CHUNK
            s = jnp.dot(wm[hd], vb[r0:r0 + GMLP_CHUNK, c0:c0 + GMLP_CHUNK], preferred_element_type=F32)
            cols.append(s + bs_ref[hd])
        rows.append(jnp.concatenate(cols, axis=1))
    yg = u * jnp.concatenate(rows, axis=0)
    o_ref[...] = _rms(yg, gout_ref[...]).astype(BF16)


def _gmlp(h, gmix, win, gv, ws, bs_full, gout, *, tm):
    m, d = h.shape
    dg = gv.shape[0]
    ublk = (win.shape[1] - 2 * dg) // dg
    kern = functools.partial(_gmlp_kernel, tm=tm, dg=dg)
    return pl.pallas_call(
        kern,
        grid=(m // tm,),
        in_specs=[
            pl.BlockSpec((tm, d), lambda i: (i, 0)),
            _resident((1, d)),
            pl.BlockSpec((d, dg), lambda i: (0, ublk), pipeline_mode=pl.Buffered(1)),
            pl.BlockSpec((d, dg), lambda i: (0, ublk + 1), pipeline_mode=pl.Buffered(1)),
            _resident((1, dg)),
            _resident(ws.shape),
            _resident(bs_full.shape),
            _resident((1, dg)),
        ],
        out_specs=pl.BlockSpec((tm, dg), lambda i: (i, 0)),
        out_shape=jax.ShapeDtypeStruct((m, dg), BF16),
        compiler_params=pltpu.CompilerParams(
            dimension_semantics=("parallel",),
            vmem_limit_bytes=48 << 20,
        ),
        name="gmlp",
    )(h, gmix.reshape(1, d), win, win, gv.reshape(1, dg), ws, bs_full, gout.reshape(1, dg))


def _outproj_kernel(h_ref, ys_ref, yg_ref, wo_ref, o_ref, *, ds):
    acc = jnp.dot(ys_ref[...], wo_ref[:ds, :], preferred_element_type=F32)
    acc += jnp.dot(yg_ref[...], wo_ref[ds:, :], preferred_element_type=F32)
    o_ref[...] = h_ref[...] + acc


def _outproj(h, ys, yg, wo, *, tm):
    m, d = h.shape
    ds = ys.shape[1]
    dg = yg.shape[1]
    kern = functools.partial(_outproj_kernel, ds=ds)
    return pl.pallas_call(
        kern,
        grid=(m // tm,),
        in_specs=[
            pl.BlockSpec((tm, d), lambda i: (i, 0)),
            pl.BlockSpec((tm, ds), lambda i: (i, 0)),
            pl.BlockSpec((tm, dg), lambda i: (i, 0)),
            _resident(wo.shape),
        ],
        out_specs=pl.BlockSpec((tm, d), lambda i: (i, 0)),
        out_shape=jax.ShapeDtypeStruct((m, d), F32),
        compiler_params=pltpu.CompilerParams(
            dimension_semantics=("parallel",),
            vmem_limit_bytes=48 << 20,
        ),
        name="outproj",
    )(h, ys, yg, wo)


def _ple_kernel(h_ref, p_ref, gple_ref, wg_ref, wp_ref, gfin_ref, o_ref):
    h = h_ref[...]
    hn = _rms(h, gple_ref[...]).astype(BF16)
    gate = jax.nn.sigmoid(jnp.dot(hn, wg_ref[...], preferred_element_type=F32))
    pp = jnp.dot(p_ref[...].astype(BF16), wp_ref[...], preferred_element_type=F32)
    o_ref[...] = _rms(h + gate * pp, gfin_ref[...])


def _ple(h, p, gple, wg, wp, gfin, *, tm):
    m, d = h.shape
    dp = p.shape[1]
    return pl.pallas_call(
        _ple_kernel,
        grid=(m // tm,),
        in_specs=[
            pl.BlockSpec((tm, d), lambda i: (i, 0)),
            pl.BlockSpec((tm, dp), lambda i: (i, 0)),
            _resident((1, d)),
            _resident(wg.shape),
            _resident(wp.shape),
            _resident((1, d)),
        ],
        out_specs=pl.BlockSpec((tm, d), lambda i: (i, 0)),
        out_shape=jax.ShapeDtypeStruct((m, d), F32),
        compiler_params=pltpu.CompilerParams(
            dimension_semantics=("parallel",),
            vmem_limit_bytes=48 << 20,
        ),
        name="ple",
    )(h, p, gple.reshape(1, d), wg, wp, gfin.reshape(1, d))


def kernel(x, p, norm_ffn1, w1_gate, w1_up, w1_down, norm_mix, w_in, ssm_log_dt, ssm_a_re, ssm_a_im, ssm_b_re, ssm_b_im, ssm_c_re, ssm_c_im, ssm_d, ssm_w_glu, gmlp_norm_v, gmlp_w_s, gmlp_b_s, norm_ssm_out, norm_gmlp_out, w_out, norm_ffn2, w2_gate, w2_up, w2_down, norm_ple, w_ple_gate, w_ple_proj, norm_final):
    bsz, seqlen, d = x.shape
    depth = p.shape[0]
    m = bsz * seqlen
    dssm = ssm_d.shape[1]
    assert depth == 1, "the per-layer embedding kernel fuses the final norm"
    h = x.reshape(m, d)
    for i in range(depth):
        nrow, nf = m // FFN_TM, w2_gate.shape[2] // FFN_TF
        jobs = [_slab_job(w, nrow * nf, lambda r, c: r * nf + c)
                for w in (w2_gate[i], w2_up[i], w2_down[i], w_in[i], ssm_w_glu[i])]
        h, (w2g, w2u, w2d, w_in_b, wglu_b) = _ffn(
            h, norm_ffn1[i], w1_gate[i].astype(BF16), w1_up[i].astype(BF16), w1_down[i].astype(BF16),
            jobs, tm=FFN_TM, tf=FFN_TF)

        abr, abi, bbr, bbi, cimn = _s5_prep(ssm_log_dt[i], ssm_a_re[i], ssm_a_im[i],
                                            ssm_b_re[i], ssm_b_im[i], ssm_c_im[i])
        bblk = jnp.concatenate([_block_diag_in(bbr), _block_diag_in(bbi)], axis=2).astype(BF16)
        cblk = jnp.concatenate([_block_diag_out(ssm_c_re[i]), _block_diag_out(cimn)], axis=1).astype(BF16)
        nblk = bblk.shape[0]
        ab_shape = (nblk, V7X_SUBLANES, V7X_LANES)
        nsteps = bsz * (seqlen // S5_TC) + 1
        jobs = [_slab_job(w, nsteps) for w in (w_out[i], w_ple_gate[i], w_ple_proj[i])]
        ys, (w_out_b, wpg_b, wpp_b) = _s5(
            h.reshape(bsz, seqlen, d), norm_mix[i], w_in_b, bblk, cblk,
            abr.reshape(ab_shape), abi.reshape(ab_shape), ssm_d[i], wglu_b, norm_ssm_out[i], jobs, tc=S5_TC)

        nh, ck = gmlp_b_s.shape[1:]
        bs_full = jnp.broadcast_to(gmlp_b_s[i][:, :, None], (nh, ck, ck))
        yg = _gmlp(h, norm_mix[i], w_in_b, gmlp_norm_v[i], gmlp_w_s[i], bs_full, norm_gmlp_out[i], tm=ROW_TM)

        h = _outproj(h, ys.reshape(m, dssm), yg, w_out_b, tm=ROW_TM)

        h = _ffn_stream(h, norm_ffn2[i], w2g, w2u, w2d, tm=FFN_TM, tf=FFN_TF)

        h = _ple(h, p[i].reshape(m, -1), norm_ple[i], wpg_b, wpp_b, norm_final, tm=ROW_TM)
    return h.reshape(bsz, seqlen, d)
```

```python
import functools
from typing import Callable, NamedTuple

import jax
import jax.numpy as jnp
from jax import lax
from jax.experimental import pallas as pl
from jax.experimental.pallas import tpu as pltpu

F32 = jnp.float32
BF16 = jnp.bfloat16
EPS = 1e-6

V7X_LANES = 128
V7X_SUBLANES = 8
V7X_VMEM_BYTES = 64 * 1024 * 1024

SSM_GROUP = 16
SSM_STATE = 64
SSM_BLOCK_GROUPS = 16
GMLP_CHUNK = 128
SCAN_ROW_PITCH = 12

FFN_TM = 1024
FFN_TF = 512
S5_TC = 256
ROW_TM = 512


def _rms(x, g):
    ms = jnp.mean(x * x, axis=-1, keepdims=True)
    return x * lax.rsqrt(ms + EPS) * g


def _resident(shape):
    n = len(shape)
    return pl.BlockSpec(shape, lambda *_: (0,) * n, pipeline_mode=pl.Buffered(1))


class _CastJob(NamedTuple):
    src: jax.Array
    block: tuple
    index_map: Callable


def _cast_specs(jobs):
    specs = [pl.BlockSpec(j.block, j.index_map) for j in jobs]
    shapes = [jax.ShapeDtypeStruct(j.src.shape, BF16) for j in jobs]
    return specs, shapes


def _run_casts(srcs, dsts):
    for src, dst in zip(srcs, dsts):
        dst[...] = src[...].astype(BF16)


BF16_ROWS = 2 * V7X_SUBLANES


def _slab_job(w, nsteps, step_of=lambda s: s):
    r, c = w.shape
    rows = BF16_ROWS
    while r // rows > nsteps:
        rows *= 2
    n = r // rows
    assert r % rows == 0
    return _CastJob(w, (rows, c), lambda *g: (jnp.minimum(step_of(*g), n - 1), 0))


def _ffn_kernel(*refs, n_cast):
    n = n_cast
    x_ref, g_ref, wg_ref, wu_ref, wd_ref = refs[:5]
    cast_src, o_ref, cast_dst, xn_ref = refs[5:5 + n], refs[5 + n], refs[6 + n:6 + 2 * n], refs[6 + 2 * n]

    @pl.when(pl.program_id(1) == 0)
    def _():
        x = x_ref[...]
        xn_ref[...] = _rms(x, g_ref[...]).astype(BF16)
        o_ref[...] = x

    _run_casts(cast_src, cast_dst)
    xn = xn_ref[...]
    gate = jnp.dot(xn, wg_ref[...], preferred_element_type=F32)
    up = jnp.dot(xn, wu_ref[...], preferred_element_type=F32)
    act = (gate * jax.nn.sigmoid(gate) * (0.5 * up)).astype(BF16)
    o_ref[...] += jnp.dot(act, wd_ref[...], preferred_element_type=F32)


def _ffn(x, g, wg, wu, wd, casts=(), *, tm, tf):
    m, d = x.shape
    f = wg.shape[1]
    row = pl.BlockSpec((tm, d), lambda i, j: (i, 0))
    cast_specs, cast_shapes = _cast_specs(casts)
    vmem = (2 * tm * d * 4) * 2 + tm * d * 2 + 3 * 2 * d * tf * 2 + 4 * tm * tf * 4
    vmem += sum(c.block[0] * c.block[1] * 2 * (4 + 2) for c in casts)
    outs = pl.pallas_call(
        functools.partial(_ffn_kernel, n_cast=len(casts)),
        grid=(m // tm, f // tf),
        in_specs=[
            row,
            pl.BlockSpec((1, d), lambda i, j: (0, 0)),
            pl.BlockSpec((d, tf), lambda i, j: (0, j)),
            pl.BlockSpec((d, tf), lambda i, j: (0, j)),
            pl.BlockSpec((tf, d), lambda i, j: (j, 0)),
        ] + cast_specs,
        out_specs=[row] + cast_specs,
        out_shape=[jax.ShapeDtypeStruct((m, d), F32)] + cast_shapes,
        scratch_shapes=[pltpu.VMEM((tm, d), BF16)],
        compiler_params=pltpu.CompilerParams(
            dimension_semantics=("parallel", "arbitrary"),
            vmem_limit_bytes=min(vmem + (8 << 20), V7X_VMEM_BYTES - (2 << 20)),
        ),
        name="ffn",
    )(x, g.reshape(1, d), wg, wu, wd, *[c.src for c in casts])
    return outs[0], list(outs[1:])


def _ffn_stream_kernel(x_ref, g_ref, wg_hbm, wu_hbm, wd_hbm, o_ref, xn_ref, wg_buf, wu_buf, wd_buf, sem,
                       *, nf, tf):
    def copies(f, slot):
        lo = pl.multiple_of(f * tf, tf)
        return (pltpu.make_async_copy(wg_hbm.at[:, pl.ds(lo, tf)], wg_buf.at[slot], sem.at[0, slot]),
                pltpu.make_async_copy(wu_hbm.at[:, pl.ds(lo, tf)], wu_buf.at[slot], sem.at[1, slot]),
                pltpu.make_async_copy(wd_hbm.at[pl.ds(lo, tf), :], wd_buf.at[slot], sem.at[2, slot]))

    def start(f, slot):
        for c in copies(f, slot):
            c.start(priority=1)

    def finish(f, slot):
        for c in copies(f, slot):
            c.wait()
        xn = xn_ref[...]
        gate = jnp.dot(xn, wg_buf[slot], preferred_element_type=F32)
        up = jnp.dot(xn, wu_buf[slot], preferred_element_type=F32)
        act = (gate * jax.nn.sigmoid(gate) * (0.5 * up)).astype(BF16)
        o_ref[...] += jnp.dot(act, wd_buf[slot], preferred_element_type=F32)

    start(0, 0)
    x = x_ref[...]
    xn_ref[...] = _rms(x, g_ref[...]).astype(BF16)
    o_ref[...] = x

    def pair(k, carry):
        start(2 * k + 1, 1)
        finish(2 * k, 0)
        start(2 * k + 2, 0)
        finish(2 * k + 1, 1)
        return carry

    lax.fori_loop(0, nf // 2, pair, 0)
    finish(nf - 1, 0)


def _ffn_stream(x, g, wg, wu, wd, *, tm, tf):
    m, d = x.shape
    f = wg.shape[1]
    nf = f // tf
    assert f % tf == 0 and nf % 2 == 1, "the tile-pair loop expects an odd number of hidden tiles"
    row = pl.BlockSpec((tm, d), lambda i: (i, 0))
    hbm = pl.BlockSpec(memory_space=pl.ANY)
    vmem = (2 * tm * d * 4) * 2 + tm * d * 2 + 3 * 2 * d * tf * 2 + 4 * tm * tf * 4
    return pl.pallas_call(
        functools.partial(_ffn_stream_kernel, nf=nf, tf=tf),
        grid=(m // tm,),
        in_specs=[row, pl.BlockSpec((1, d), lambda i: (0, 0)), hbm, hbm, hbm],
        out_specs=row,
        out_shape=jax.ShapeDtypeStruct((m, d), F32),
        scratch_shapes=[
            pltpu.VMEM((tm, d), BF16),
            pltpu.VMEM((2, d, tf), BF16),
            pltpu.VMEM((2, d, tf), BF16),
            pltpu.VMEM((2, tf, d), BF16),
            pltpu.SemaphoreType.DMA((3, 2)),
        ],
        compiler_params=pltpu.CompilerParams(
            dimension_semantics=("arbitrary",),
            vmem_limit_bytes=min(vmem + (8 << 20), V7X_VMEM_BYTES - (2 << 20)),
        ),
        name="ffn_stream",
    )(x, g.reshape(1, d), wg, wu, wd)


def _zoh(logdt, a_re, a_im):
    dt = jnp.exp(logdt)
    lr = jnp.minimum(a_re, -1e-4)
    li = a_im
    mag = jnp.exp(lr * dt)
    ang = li * dt
    abr = mag * jnp.cos(ang)
    abi = mag * jnp.sin(ang)
    den = lr * lr + li * li
    xr = abr - 1.0
    xi = abi
    zr = (xr * lr + xi * li) / den
    zi = (xi * lr - xr * li) / den
    return abr, abi, zr, zi


def _s5_prep_kernel(ldt_ref, are_ref, aim_ref, ldtb_ref, areb_ref, aimb_ref, bre_ref, bim_ref, cim_ref,
                    abr_ref, abi_ref, bbr_ref, bbi_ref, cimn_ref):
    abr, abi, _, _ = _zoh(ldt_ref[...], are_ref[...], aim_ref[...])
    abr_ref[...] = abr
    abi_ref[...] = abi
    _, _, zr, zi = _zoh(ldtb_ref[...], areb_ref[...], aimb_ref[...])
    br = bre_ref[...]
    bi = bim_ref[...]
    bbr_ref[...] = zr * br - zi * bi
    bbi_ref[...] = zr * bi + zi * br
    cimn_ref[...] = -cim_ref[...]


def _s5_prep(log_dt, a_re, a_im, b_re, b_im, c_im):
    g, n = a_re.shape
    p = b_re.shape[2]
    small = (g * n // V7X_LANES, V7X_LANES)
    big = (g * p * n // V7X_LANES, V7X_LANES)

    def bc(a):
        return jnp.broadcast_to(a[:, None, :], (g, p, n)).reshape(big)

    ldt = jnp.broadcast_to(log_dt[:, None], (g, n))
    outs = pl.pallas_call(
        _s5_prep_kernel,
        out_shape=[jax.ShapeDtypeStruct(small, F32)] * 2 + [jax.ShapeDtypeStruct(big, F32)] * 3,
        name="s5_prep",
    )(ldt.reshape(small), a_re.reshape(small), a_im.reshape(small),
      bc(ldt), bc(a_re), bc(a_im),
      jnp.swapaxes(b_re, 1, 2).reshape(big), jnp.swapaxes(b_im, 1, 2).reshape(big), c_im.reshape(big))
    abr, abi, bbr, bbi, cimn = outs
    return (abr.reshape(g, n), abi.reshape(g, n),
            bbr.reshape(g, p, n), bbi.reshape(g, p, n), cimn.reshape(g, p, n))


def _block_diag_in(w):
    g, p, n = w.shape
    nb = g // SSM_BLOCK_GROUPS
    w4 = w.reshape(nb, SSM_BLOCK_GROUPS, p, n)
    eye = jnp.eye(SSM_BLOCK_GROUPS, dtype=w.dtype)
    return jnp.einsum("kipn,ij->kipjn", w4, eye).reshape(nb, SSM_BLOCK_GROUPS * p, SSM_BLOCK_GROUPS * n)


def _block_diag_out(w):
    g, p, n = w.shape
    nb = g // SSM_BLOCK_GROUPS
    w4 = w.reshape(nb, SSM_BLOCK_GROUPS, p, n)
    eye = jnp.eye(SSM_BLOCK_GROUPS, dtype=w.dtype)
    return jnp.einsum("kipn,ij->kinjp", w4, eye).reshape(nb, SSM_BLOCK_GROUPS * n, SSM_BLOCK_GROUPS * p)


def _s5_kernel(*refs, tc, nblk, bch, bst, chunks_per_seq, n_cast):
    h_ref, gmix_ref, win_ref, bblk_ref, cblk_ref, abr_ref, abi_ref, d_ref, wglu_ref, gout_ref = refs[:10]
    cast_src, o_ref, cast_dst = refs[10:10 + n_cast], refs[10 + n_cast], refs[11 + n_cast:11 + 2 * n_cast]
    dre0_ref, dim0_ref, u0_ref, dre1_ref, dim1_ref, u1_ref, st_ref = refs[11 + 2 * n_cast:]
    ntile = bst // V7X_LANES
    nb = tc // V7X_SUBLANES
    step_id = pl.program_id(0)
    bufs = ((dre0_ref, dim0_ref, u0_ref), (dre1_ref, dim1_ref, u1_ref))
    _run_casts(cast_src, cast_dst)

    @pl.when(step_id == 0)
    def _():
        st_ref[...] = jnp.zeros_like(st_ref)
        for ref in bufs[1]:
            ref[...] = jnp.zeros_like(ref)

    def step(wr, rd):
        dre_w, dim_w, u_w = wr
        dre_r, dim_r, u_r = rd

        keep = jnp.where((step_id - 1) % chunks_per_seq == 0, 0.0, 1.0).astype(F32)
        ar = [abr_ref[k] for k in range(nblk)]
        ai = [abi_ref[k] for k in range(nblk)]
        s = [st_ref[i] * keep for i in range(2 * nblk)]
        for tb in range(nb):
            for r in range(V7X_SUBLANES):
                rows = slice(r * SCAN_ROW_PITCH, r * SCAN_ROW_PITCH + ntile)
                for k in range(nblk):
                    sre, sim = s[2 * k], s[2 * k + 1]
                    nre = ar[k] * sre - ai[k] * sim + dre_r[k, tb, rows, :]
                    nim = ar[k] * sim + ai[k] * sre + dim_r[k, tb, rows, :]
                    dre_r[k, tb, rows, :] = nre
                    dim_r[k, tb, rows, :] = nim
                    s[2 * k], s[2 * k + 1] = nre, nim
        for i in range(2 * nblk):
            st_ref[i] = s[i]

        hn = _rms(h_ref[0], gmix_ref[...]).astype(BF16)
        u = jnp.dot(hn, win_ref[...], preferred_element_type=F32)
        u_w[...] = u
        ub = u.astype(BF16)
        for k in range(nblk):
            drv = jnp.dot(ub[:, k * bch:(k + 1) * bch], bblk_ref[k], preferred_element_type=F32)
            for j in range(ntile):
                lo = j * V7X_LANES
                tile = pl.ds(j, V7X_SUBLANES, stride=SCAN_ROW_PITCH)
                dre_w[k, :, tile, :] = drv[:, lo:lo + V7X_LANES].reshape(nb, V7X_SUBLANES, V7X_LANES)
                dim_w[k, :, tile, :] = drv[:, bst + lo:bst + lo + V7X_LANES].reshape(nb, V7X_SUBLANES, V7X_LANES)

        ys = []
        for k in range(nblk):
            parts = []
            for ref in (dre_r, dim_r):
                for j in range(ntile):
                    tile = pl.ds(j, V7X_SUBLANES, stride=SCAN_ROW_PITCH)
                    parts.append(ref[k, :, tile, :].reshape(tc, V7X_LANES))
            lhs = jnp.concatenate(parts, axis=1).astype(BF16)
            ys.append(jnp.dot(lhs, cblk_ref[k], preferred_element_type=F32))
        y = jnp.concatenate(ys, axis=1) + d_ref[...] * u_r[...]
        y = jax.nn.gelu(y)
        y = y * jax.nn.sigmoid(jnp.dot(y.astype(BF16), wglu_ref[...], preferred_element_type=F32))
        o_ref[0] = _rms(y, gout_ref[...]).astype(BF16)

    @pl.when(step_id % 2 == 0)
    def _():
        step(bufs[0], bufs[1])

    @pl.when(step_id % 2 == 1)
    def _():
        step(bufs[1], bufs[0])


def _s5(h3, gmix, win, bblk, cblk, abr, abi, dskip, wglu, gout, casts=(), *, tc):
    b, l, d = h3.shape
    dssm = dskip.shape[0]
    nblk, bch, bst2 = bblk.shape
    bst = bst2 // 2
    ntile = bst // V7X_LANES
    assert ntile == V7X_SUBLANES and tc % V7X_SUBLANES == 0
    nb = tc // V7X_SUBLANES
    nch = l // tc
    last = b * nch - 1
    kern = functools.partial(_s5_kernel, tc=tc, nblk=nblk, bch=bch, bst=bst, chunks_per_seq=nch,
                             n_cast=len(casts))
    cast_specs, cast_shapes = _cast_specs(casts)
    dbuf = pltpu.VMEM((nblk, nb, V7X_SUBLANES * SCAN_ROW_PITCH, V7X_LANES), F32)
    ubuf = pltpu.VMEM((tc, dssm), F32)

    def in_chunk(s):
        c = jnp.minimum(s, last)
        return (c // nch, c % nch, 0)

    def out_chunk(s):
        c = jnp.maximum(s - 1, 0)
        return (c // nch, c % nch, 0)

    outs = pl.pallas_call(
        kern,
        grid=(b * nch + 1,),
        in_specs=[
            pl.BlockSpec((1, tc, d), in_chunk),
            _resident((1, d)),
            pl.BlockSpec((d, dssm), lambda s: (0, 0), pipeline_mode=pl.Buffered(1)),
            _resident(bblk.shape),
            _resident(cblk.shape),
            _resident(abr.shape),
            _resident(abi.shape),
            _resident((1, dssm)),
            _resident(wglu.shape),
            _resident((1, dssm)),
        ] + cast_specs,
        out_specs=[pl.BlockSpec((1, tc, dssm), out_chunk)] + cast_specs,
        out_shape=[jax.ShapeDtypeStruct((b, l, dssm), BF16)] + cast_shapes,
        scratch_shapes=[
            dbuf, dbuf, ubuf, dbuf, dbuf, ubuf,
            pltpu.VMEM((2 * nblk, V7X_SUBLANES, V7X_LANES), F32),
        ],
        compiler_params=pltpu.CompilerParams(
            dimension_semantics=("arbitrary",),
            vmem_limit_bytes=56 << 20,
        ),
        name="s5",
    )(h3, gmix.reshape(1, d), win, bblk, cblk, abr, abi, dskip.reshape(1, dssm), wglu, gout.reshape(1, dssm),
      *[c.src for c in casts])
    return outs[0], list(outs[1:])


def _gmlp_kernel(h_ref, gmix_ref, wu_ref, wv_ref, gv_ref, ws_ref, bs_ref, gout_ref, o_ref, *, tm, dg):
    hn = _rms(h_ref[...], gmix_ref[...]).astype(BF16)
    zu = jnp.dot(hn, wu_ref[...], preferred_element_type=F32)
    zv = jnp.dot(hn, wv_ref[...], preferred_element_type=F32)
    u = jax.nn.gelu(zu)
    v = jax.nn.gelu(zv)
    vc = v - jnp.mean(v, axis=-1, keepdims=True)
    vn = vc * lax.rsqrt(jnp.mean(vc * vc, axis=-1, keepdims=True) + EPS) * gv_ref[...]
    vb = vn.astype(BF16)

    nh = dg // GMLP_CHUNK
    t_idx = lax.broadcasted_iota(jnp.int32, (GMLP_CHUNK, GMLP_CHUNK), 0)
    s_idx = lax.broadcasted_iota(jnp.int32, (GMLP_CHUNK, GMLP_CHUNK), 1)
    causal = t_idx >= s_idx
    wm = [jnp.where(causal, ws_ref[hd], 0.0).astype(BF16) for hd in range(nh)]
    rows = []
    for c in range(tm // GMLP_CHUNK):
        r0 = c * GMLP_CHUNK
        cols = []
        for hd in range(nh):
            c0 = hd * GMLP_CHUNK
            s = jnp.dot(wm[hd], vb[r0:r0 + GMLP_CHUNK, c0:c0 + GMLP_CHUNK], preferred_element_type=F32)
            cols.append(s + bs_ref[hd])
        rows.append(jnp.concatenate(cols, axis=1))
    yg = u * jnp.concatenate(rows, axis=0)
    o_ref[...] = _rms(yg, gout_ref[...]).astype(BF16)


def _gmlp(h, gmix, win, gv, ws, bs_full, gout, *, tm):
    m, d = h.shape
    dg = gv.shape[0]
    ublk = (win.shape[1] - 2 * dg) // dg
    kern = functools.partial(_gmlp_kernel, tm=tm, dg=dg)
    return pl.pallas_call(
        kern,
        grid=(m // tm,),
        in_specs=[
            pl.BlockSpec((tm, d), lambda i: (i, 0)),
            _resident((1, d)),
            pl.BlockSpec((d, dg), lambda i: (0, ublk), pipeline_mode=pl.Buffered(1)),
            pl.BlockSpec((d, dg), lambda i: (0, ublk + 1), pipeline_mode=pl.Buffered(1)),
            _resident((1, dg)),
            _resident(ws.shape),
            _resident(bs_full.shape),
            _resident((1, dg)),
        ],
        out_specs=pl.BlockSpec((tm, dg), lambda i: (i, 0)),
        out_shape=jax.ShapeDtypeStruct((m, dg), BF16),
        compiler_params=pltpu.CompilerParams(
            dimension_semantics=("parallel",),
            vmem_limit_bytes=48 << 20,
        ),
        name="gmlp",
    )(h, gmix.reshape(1, d), win, win, gv.reshape(1, dg), ws, bs_full, gout.reshape(1, dg))


def _outproj_kernel(h_ref, ys_ref, yg_ref, wo_ref, o_ref, *, ds):
    acc = jnp.dot(ys_ref[...], wo_ref[:ds, :], preferred_element_type=F32)
    acc += jnp.dot(yg_ref[...], wo_ref[ds:, :], preferred_element_type=F32)
    o_ref[...] = h_ref[...] + acc


def _outproj(h, ys, yg, wo, *, tm):
    m, d = h.shape
    ds = ys.shape[1]
    dg = yg.shape[1]
    kern = functools.partial(_outproj_kernel, ds=ds)
    return pl.pallas_call(
        kern,
        grid=(m // tm,),
        in_specs=[
            pl.BlockSpec((tm, d), lambda i: (i, 0)),
            pl.BlockSpec((tm, ds), lambda i: (i, 0)),
            pl.BlockSpec((tm, dg), lambda i: (i, 0)),
            _resident(wo.shape),
        ],
        out_specs=pl.BlockSpec((tm, d), lambda i: (i, 0)),
        out_shape=jax.ShapeDtypeStruct((m, d), F32),
        compiler_params=pltpu.CompilerParams(
            dimension_semantics=("parallel",),
            vmem_limit_bytes=48 << 20,
        ),
        name="outproj",
    )(h, ys, yg, wo)


def _ple_kernel(h_ref, p_ref, gple_ref, wg_ref, wp_ref, gfin_ref, o_ref):
    h = h_ref[...]
    hn = _rms(h, gple_ref[...]).astype(BF16)
    gate = jax.nn.sigmoid(jnp.dot(hn, wg_ref[...], preferred_element_type=F32))
    pp = jnp.dot(p_ref[...].astype(BF16), wp_ref[...], preferred_element_type=F32)
    o_ref[...] = _rms(h + gate * pp, gfin_ref[...])


def _ple(h, p, gple, wg, wp, gfin, *, tm):
    m, d = h.shape
    dp = p.shape[1]
    return pl.pallas_call(
        _ple_kernel,
        grid=(m // tm,),
        in_specs=[
            pl.BlockSpec((tm, d), lambda i: (i, 0)),
            pl.BlockSpec((tm, dp), lambda i: (i, 0)),
            _resident((1, d)),
            _resident(wg.shape),
            _resident(wp.shape),
            _resident((1, d)),
        ],
        out_specs=pl.BlockSpec((tm, d), lambda i: (i, 0)),
        out_shape=jax.ShapeDtypeStruct((m, d), F32),
        compiler_params=pltpu.CompilerParams(
            dimension_semantics=("parallel",),
            vmem_limit_bytes=48 << 20,
        ),
        name="ple",
    )(h, p, gple.reshape(1, d), wg, wp, gfin.reshape(1, d))


def kernel(x, p, norm_ffn1, w1_gate, w1_up, w1_down, norm_mix, w_in, ssm_log_dt, ssm_a_re, ssm_a_im, ssm_b_re, ssm_b_im, ssm_c_re, ssm_c_im, ssm_d, ssm_w_glu, gmlp_norm_v, gmlp_w_s, gmlp_b_s, norm_ssm_out, norm_gmlp_out, w_out, norm_ffn2, w2_gate, w2_up, w2_down, norm_ple, w_ple_gate, w_ple_proj, norm_final):
    bsz, seqlen, d = x.shape
    depth = p.shape[0]
    m = bsz * seqlen
    dssm = ssm_d.shape[1]
    assert depth == 1, "the per-layer embedding kernel fuses the final norm"
    h = x.reshape(m, d)
    for i in range(depth):
        nrow, nf = m // FFN_TM, w2_gate.shape[2] // FFN_TF
        jobs = [_slab_job(w, nrow * nf, lambda r, c: r * nf + c)
                for w in (w2_gate[i], w2_up[i], w2_down[i], w_in[i], ssm_w_glu[i])]
        h, (w2g, w2u, w2d, w_in_b, wglu_b) = _ffn(
            h, norm_ffn1[i], w1_gate[i].astype(BF16), w1_up[i].astype(BF16), w1_down[i].astype(BF16),
            jobs, tm=FFN_TM, tf=FFN_TF)

        abr, abi, bbr, bbi, cimn = _s5_prep(ssm_log_dt[i], ssm_a_re[i], ssm_a_im[i],
                                            ssm_b_re[i], ssm_b_im[i], ssm_c_im[i])
        bblk = jnp.concatenate([_block_diag_in(bbr), _block_diag_in(bbi)], axis=2).astype(BF16)
        cblk = jnp.concatenate([_block_diag_out(ssm_c_re[i]), _block_diag_out(cimn)], axis=1).astype(BF16)
        nblk = bblk.shape[0]
        ab_shape = (nblk, V7X_SUBLANES, V7X_LANES)
        nsteps = bsz * (seqlen // S5_TC) + 1
        jobs = [_slab_job(w, nsteps) for w in (w_out[i], w_ple_gate[i], w_ple_proj[i])]
        ys, (w_out_b, wpg_b, wpp_b) = _s5(
            h.reshape(bsz, seqlen, d), norm_mix[i], w_in_b, bblk, cblk,
            abr.reshape(ab_shape), abi.reshape(ab_shape), ssm_d[i], wglu_b, norm_ssm_out[i], jobs, tc=S5_TC)

        nh, ck = gmlp_b_s.shape[1:]
        bs_full = jnp.broadcast_to(gmlp_b_s[i][:, :, None], (nh, ck, ck))
        yg = _gmlp(h, norm_mix[i], w_in_b, gmlp_norm_v[i], gmlp_w_s[i], bs_full, norm_gmlp_out[i], tm=ROW_TM)

        h = _outproj(h, ys.reshape(m, dssm), yg, w_out_b, tm=ROW_TM)

        h = _ffn_stream(h, norm_ffn2[i], w2g, w2u, w2d, tm=FFN_TM, tf=FFN_TF)

        h = _ple(h, p[i].reshape(m, -1), norm_ple[i], wpg_b, wpp_b, norm_final, tm=ROW_TM)
    return h.reshape(bsz, seqlen, d)
```

```python
import functools
from typing import Callable, NamedTuple

import jax
import jax.numpy as jnp
from jax import lax
from jax.experimental import pallas as pl
from jax.experimental.pallas import tpu as pltpu

F32 = jnp.float32
BF16 = jnp.bfloat16
EPS = 1e-6

V7X_LANES = 128
V7X_SUBLANES = 8
V7X_VMEM_BYTES = 64 * 1024 * 1024

SSM_GROUP = 16
SSM_STATE = 64
SSM_BLOCK_GROUPS = 16
GMLP_CHUNK = 128
SCAN_ROW_PITCH = 12

FFN_TM = 1024
FFN_TF = 512
S5_TC = 256
ROW_TM = 512


def _rms(x, g):
    ms = jnp.mean(x * x, axis=-1, keepdims=True)
    return x * lax.rsqrt(ms + EPS) * g


def _resident(shape):
    n = len(shape)
    return pl.BlockSpec(shape, lambda *_: (0,) * n, pipeline_mode=pl.Buffered(1))


class _CastJob(NamedTuple):
    src: jax.Array
    block: tuple
    index_map: Callable


def _cast_specs(jobs):
    specs = [pl.BlockSpec(j.block, j.index_map) for j in jobs]
    shapes = [jax.ShapeDtypeStruct(j.src.shape, BF16) for j in jobs]
    return specs, shapes


def _run_casts(srcs, dsts):
    for src, dst in zip(srcs, dsts):
        dst[...] = src[...].astype(BF16)


BF16_ROWS = 2 * V7X_SUBLANES


def _slab_job(w, nsteps, step_of=lambda s: s):
    r, c = w.shape
    rows = BF16_ROWS
    while r // rows > nsteps:
        rows *= 2
    n = r // rows
    assert r % rows == 0
    return _CastJob(w, (rows, c), lambda *g: (jnp.minimum(step_of(*g), n - 1), 0))


def _ffn_kernel(*refs, n_cast):
    n = n_cast
    x_ref, g_ref, wg_ref, wu_ref, wd_ref = refs[:5]
    cast_src, o_ref, cast_dst, xn_ref = refs[5:5 + n], refs[5 + n], refs[6 + n:6 + 2 * n], refs[6 + 2 * n]

    @pl.when(pl.program_id(1) == 0)
    def _():
        x = x_ref[...]
        xn_ref[...] = _rms(x, g_ref[...]).astype(BF16)
        o_ref[...] = x

    _run_casts(cast_src, cast_dst)
    xn = xn_ref[...]
    gate = jnp.dot(xn, wg_ref[...], preferred_element_type=F32)
    up = jnp.dot(xn, wu_ref[...], preferred_element_type=F32)
    act = (gate * jax.nn.sigmoid(gate) * (0.5 * up)).astype(BF16)
    o_ref[...] += jnp.dot(act, wd_ref[...], preferred_element_type=F32)


def _ffn(x, g, wg, wu, wd, casts=(), *, tm, tf):
    m, d = x.shape
    f = wg.shape[1]
    row = pl.BlockSpec((tm, d), lambda i, j: (i, 0))
    cast_specs, cast_shapes = _cast_specs(casts)
    vmem = (2 * tm * d * 4) * 2 + tm * d * 2 + 3 * 2 * d * tf * 2 + 4 * tm * tf * 4
    vmem += sum(c.block[0] * c.block[1] * 2 * (4 + 2) for c in casts)
    outs = pl.pallas_call(
        functools.partial(_ffn_kernel, n_cast=len(casts)),
        grid=(m // tm, f // tf),
        in_specs=[
            row,
            pl.BlockSpec((1, d), lambda i, j: (0, 0)),
            pl.BlockSpec((d, tf), lambda i, j: (0, j)),
            pl.BlockSpec((d, tf), lambda i, j: (0, j)),
            pl.BlockSpec((tf, d), lambda i, j: (j, 0)),
        ] + cast_specs,
        out_specs=[row] + cast_specs,
        out_shape=[jax.ShapeDtypeStruct((m, d), F32)] + cast_shapes,
        scratch_shapes=[pltpu.VMEM((tm, d), BF16)],
        compiler_params=pltpu.CompilerParams(
            dimension_semantics=("parallel", "arbitrary"),
            vmem_limit_bytes=min(vmem + (8 << 20), V7X_VMEM_BYTES - (2 << 20)),
        ),
        name="ffn",
    )(x, g.reshape(1, d), wg, wu, wd, *[c.src for c in casts])
    return outs[0], list(outs[1:])


def _ffn_stream_kernel(x_ref, g_ref, wg_hbm, wu_hbm, wd_hbm, o_ref, xn_ref, wg_buf, wu_buf, wd_buf, sem,
                       *, nf, tf):
    def copies(f, slot):
        lo = pl.multiple_of(f * tf, tf)
        return (pltpu.make_async_copy(wg_hbm.at[:, pl.ds(lo, tf)], wg_buf.at[slot], sem.at[0, slot]),
                pltpu.make_async_copy(wu_hbm.at[:, pl.ds(lo, tf)], wu_buf.at[slot], sem.at[1, slot]),
                pltpu.make_async_copy(wd_hbm.at[pl.ds(lo, tf), :], wd_buf.at[slot], sem.at[2, slot]))

    def start(f, slot):
        for c in copies(f, slot):
            c.start(priority=1)

    def finish(f, slot):
        for c in copies(f, slot):
            c.wait()
        xn = xn_ref[...]
        gate = jnp.dot(xn, wg_buf[slot], preferred_element_type=F32)
        up = jnp.dot(xn, wu_buf[slot], preferred_element_type=F32)
        act = (gate * jax.nn.sigmoid(gate) * (0.5 * up)).astype(BF16)
        o_ref[...] += jnp.dot(act, wd_buf[slot], preferred_element_type=F32)

    start(0, 0)
    x = x_ref[...]
    xn_ref[...] = _rms(x, g_ref[...]).astype(BF16)
    o_ref[...] = x

    def tile(f, carry):
        slot = f % 2

        @pl.when(f + 1 < nf)
        def _():
            start(f + 1, 1 - slot)

        finish(f, slot)
        return carry

    lax.fori_loop(0, nf, tile, 0)


def _ffn_stream(x, g, wg, wu, wd, *, tm, tf):
    m, d = x.shape
    f = wg.shape[1]
    nf = f // tf
    assert f % tf == 0
    row = pl.BlockSpec((tm, d), lambda i: (i, 0))
    hbm = pl.BlockSpec(memory_space=pl.ANY)
    vmem = (2 * tm * d * 4) * 2 + tm * d * 2 + 3 * 2 * d * tf * 2 + 4 * tm * tf * 4
    return pl.pallas_call(
        functools.partial(_ffn_stream_kernel, nf=nf, tf=tf),
        grid=(m // tm,),
        in_specs=[row, pl.BlockSpec((1, d), lambda i: (0, 0)), hbm, hbm, hbm],
        out_specs=row,
        out_shape=jax.ShapeDtypeStruct((m, d), F32),
        scratch_shapes=[
            pltpu.VMEM((tm, d), BF16),
            pltpu.VMEM((2, d, tf), BF16),
            pltpu.VMEM((2, d, tf), BF16),
            pltpu.VMEM((2, tf, d), BF16),
            pltpu.SemaphoreType.DMA((3, 2)),
        ],
        compiler_params=pltpu.CompilerParams(
            dimension_semantics=("arbitrary",),
            vmem_limit_bytes=min(vmem + (8 << 20), V7X_VMEM_BYTES - (2 << 20)),
        ),
        name="ffn_stream",
    )(x, g.reshape(1, d), wg, wu, wd)


def _zoh(logdt, a_re, a_im):
    dt = jnp.exp(logdt)
    lr = jnp.minimum(a_re, -1e-4)
    li = a_im
    mag = jnp.exp(lr * dt)
    ang = li * dt
    abr = mag * jnp.cos(ang)
    abi = mag * jnp.sin(ang)
    den = lr * lr + li * li
    xr = abr - 1.0
    xi = abi
    zr = (xr * lr + xi * li) / den
    zi = (xi * lr - xr * li) / den
    return abr, abi, zr, zi


def _s5_prep_kernel(ldt_ref, are_ref, aim_ref, ldtb_ref, areb_ref, aimb_ref, bre_ref, bim_ref, cim_ref,
                    abr_ref, abi_ref, bbr_ref, bbi_ref, cimn_ref):
    abr, abi, _, _ = _zoh(ldt_ref[...], are_ref[...], aim_ref[...])
    abr_ref[...] = abr
    abi_ref[...] = abi
    _, _, zr, zi = _zoh(ldtb_ref[...], areb_ref[...], aimb_ref[...])
    br = bre_ref[...]
    bi = bim_ref[...]
    bbr_ref[...] = zr * br - zi * bi
    bbi_ref[...] = zr * bi + zi * br
    cimn_ref[...] = -cim_ref[...]


def _s5_prep(log_dt, a_re, a_im, b_re, b_im, c_im):
    g, n = a_re.shape
    p = b_re.shape[2]
    small = (g * n // V7X_LANES, V7X_LANES)
    big = (g * p * n // V7X_LANES, V7X_LANES)

    def bc(a):
        return jnp.broadcast_to(a[:, None, :], (g, p, n)).reshape(big)

    ldt = jnp.broadcast_to(log_dt[:, None], (g, n))
    outs = pl.pallas_call(
        _s5_prep_kernel,
        out_shape=[jax.ShapeDtypeStruct(small, F32)] * 2 + [jax.ShapeDtypeStruct(big, F32)] * 3,
        name="s5_prep",
    )(ldt.reshape(small), a_re.reshape(small), a_im.reshape(small),
      bc(ldt), bc(a_re), bc(a_im),
      jnp.swapaxes(b_re, 1, 2).reshape(big), jnp.swapaxes(b_im, 1, 2).reshape(big), c_im.reshape(big))
    abr, abi, bbr, bbi, cimn = outs
    return (abr.reshape(g, n), abi.reshape(g, n),
            bbr.reshape(g, p, n), bbi.reshape(g, p, n), cimn.reshape(g, p, n))


def _block_diag_in(w):
    g, p, n = w.shape
    nb = g // SSM_BLOCK_GROUPS
    w4 = w.reshape(nb, SSM_BLOCK_GROUPS, p, n)
    eye = jnp.eye(SSM_BLOCK_GROUPS, dtype=w.dtype)
    return jnp.einsum("kipn,ij->kipjn", w4, eye).reshape(nb, SSM_BLOCK_GROUPS * p, SSM_BLOCK_GROUPS * n)


def _block_diag_out(w):
    g, p, n = w.shape
    nb = g // SSM_BLOCK_GROUPS
    w4 = w.reshape(nb, SSM_BLOCK_GROUPS, p, n)
    eye = jnp.eye(SSM_BLOCK_GROUPS, dtype=w.dtype)
    return jnp.einsum("kipn,ij->kinjp", w4, eye).reshape(nb, SSM_BLOCK_GROUPS * n, SSM_BLOCK_GROUPS * p)


def _s5_kernel(*refs, tc, nblk, bch, bst, chunks_per_seq, n_cast):
    h_ref, gmix_ref, win_ref, bblk_ref, cblk_ref, abr_ref, abi_ref, d_ref, wglu_ref, gout_ref = refs[:10]
    cast_src, o_ref, cast_dst = refs[10:10 + n_cast], refs[10 + n_cast], refs[11 + n_cast:11 + 2 * n_cast]
    dre0_ref, dim0_ref, u0_ref, dre1_ref, dim1_ref, u1_ref, st_ref = refs[11 + 2 * n_cast:]
    ntile = bst // V7X_LANES
    nb = tc // V7X_SUBLANES
    step_id = pl.program_id(0)
    bufs = ((dre0_ref, dim0_ref, u0_ref), (dre1_ref, dim1_ref, u1_ref))
    _run_casts(cast_src, cast_dst)

    @pl.when(step_id == 0)
    def _():
        st_ref[...] = jnp.zeros_like(st_ref)
        for ref in bufs[1]:
            ref[...] = jnp.zeros_like(ref)

    def step(wr, rd):
        dre_w, dim_w, u_w = wr
        dre_r, dim_r, u_r = rd

        keep = jnp.where((step_id - 1) % chunks_per_seq == 0, 0.0, 1.0).astype(F32)
        ar = [abr_ref[k] for k in range(nblk)]
        ai = [abi_ref[k] for k in range(nblk)]
        s = [st_ref[i] * keep for i in range(2 * nblk)]
        for tb in range(nb):
            for r in range(V7X_SUBLANES):
                rows = slice(r * SCAN_ROW_PITCH, r * SCAN_ROW_PITCH + ntile)
                for k in range(nblk):
                    sre, sim = s[2 * k], s[2 * k + 1]
                    nre = ar[k] * sre - ai[k] * sim + dre_r[k, tb, rows, :]
                    nim = ar[k] * sim + ai[k] * sre + dim_r[k, tb, rows, :]
                    dre_r[k, tb, rows, :] = nre
                    dim_r[k, tb, rows, :] = nim
                    s[2 * k], s[2 * k + 1] = nre, nim
        for i in range(2 * nblk):
            st_ref[i] = s[i]

        hn = _rms(h_ref[0], gmix_ref[...]).astype(BF16)
        u = jnp.dot(hn, win_ref[...], preferred_element_type=F32)
        u_w[...] = u
        ub = u.astype(BF16)
        for k in range(nblk):
            drv = jnp.dot(ub[:, k * bch:(k + 1) * bch], bblk_ref[k], preferred_element_type=F32)
            for j in range(ntile):
                lo = j * V7X_LANES
                tile = pl.ds(j, V7X_SUBLANES, stride=SCAN_ROW_PITCH)
                dre_w[k, :, tile, :] = drv[:, lo:lo + V7X_LANES].reshape(nb, V7X_SUBLANES, V7X_LANES)
                dim_w[k, :, tile, :] = drv[:, bst + lo:bst + lo + V7X_LANES].reshape(nb, V7X_SUBLANES, V7X_LANES)

        ys = []
        for k in range(nblk):
            parts = []
            for ref in (dre_r, dim_r):
                for j in range(ntile):
                    tile = pl.ds(j, V7X_SUBLANES, stride=SCAN_ROW_PITCH)
                    parts.append(ref[k, :, tile, :].reshape(tc, V7X_LANES))
            lhs = jnp.concatenate(parts, axis=1).astype(BF16)
            ys.append(jnp.dot(lhs, cblk_ref[k], preferred_element_type=F32))
        y = jnp.concatenate(ys, axis=1) + d_ref[...] * u_r[...]
        y = jax.nn.gelu(y)
        y = y * jax.nn.sigmoid(jnp.dot(y.astype(BF16), wglu_ref[...], preferred_element_type=F32))
        o_ref[0] = _rms(y, gout_ref[...]).astype(BF16)

    @pl.when(step_id % 2 == 0)
    def _():
        step(bufs[0], bufs[1])

    @pl.when(step_id % 2 == 1)
    def _():
        step(bufs[1], bufs[0])


def _s5(h3, gmix, win, bblk, cblk, abr, abi, dskip, wglu, gout, casts=(), *, tc):
    b, l, d = h3.shape
    dssm = dskip.shape[0]
    nblk, bch, bst2 = bblk.shape
    bst = bst2 // 2
    ntile = bst // V7X_LANES
    assert ntile == V7X_SUBLANES and tc % V7X_SUBLANES == 0
    nb = tc // V7X_SUBLANES
    nch = l // tc
    last = b * nch - 1
    kern = functools.partial(_s5_kernel, tc=tc, nblk=nblk, bch=bch, bst=bst, chunks_per_seq=nch,
                             n_cast=len(casts))
    cast_specs, cast_shapes = _cast_specs(casts)
    dbuf = pltpu.VMEM((nblk, nb, V7X_SUBLANES * SCAN_ROW_PITCH, V7X_LANES), F32)
    ubuf = pltpu.VMEM((tc, dssm), F32)

    def in_chunk(s):
        c = jnp.minimum(s, last)
        return (c // nch, c % nch, 0)

    def out_chunk(s):
        c = jnp.maximum(s - 1, 0)
        return (c // nch, c % nch, 0)

    outs = pl.pallas_call(
        kern,
        grid=(b * nch + 1,),
        in_specs=[
            pl.BlockSpec((1, tc, d), in_chunk),
            _resident((1, d)),
            pl.BlockSpec((d, dssm), lambda s: (0, 0), pipeline_mode=pl.Buffered(1)),
            _resident(bblk.shape),
            _resident(cblk.shape),
            _resident(abr.shape),
            _resident(abi.shape),
            _resident((1, dssm)),
            _resident(wglu.shape),
            _resident((1, dssm)),
        ] + cast_specs,
        out_specs=[pl.BlockSpec((1, tc, dssm), out_chunk)] + cast_specs,
        out_shape=[jax.ShapeDtypeStruct((b, l, dssm), BF16)] + cast_shapes,
        scratch_shapes=[
            dbuf, dbuf, ubuf, dbuf, dbuf, ubuf,
            pltpu.VMEM((2 * nblk, V7X_SUBLANES, V7X_LANES), F32),
        ],
        compiler_params=pltpu.CompilerParams(
            dimension_semantics=("arbitrary",),
            vmem_limit_bytes=56 << 20,
        ),
        name="s5",
    )(h3, gmix.reshape(1, d), win, bblk, cblk, abr, abi, dskip.reshape(1, dssm), wglu, gout.reshape(1, dssm),
      *[c.src for c in casts])
    return outs[0], list(outs[1:])


def _gmlp_kernel(h_ref, gmix_ref, wu_ref, wv_ref, gv_ref, ws_ref, bs_ref, gout_ref, o_ref, *, tm, dg):
    hn = _rms(h_ref[...], gmix_ref[...]).astype(BF16)
    zu = jnp.dot(hn, wu_ref[...], preferred_element_type=F32)
    zv = jnp.dot(hn, wv_ref[...], preferred_element_type=F32)
    u = jax.nn.gelu(zu)
    v = jax.nn.gelu(zv)
    vc = v - jnp.mean(v, axis=-1, keepdims=True)
    vn = vc * lax.rsqrt(jnp.mean(vc * vc, axis=-1, keepdims=True) + EPS) * gv_ref[...]
    vb = vn.astype(BF16)

    nh = dg // GMLP_CHUNK
    t_idx = lax.broadcasted_iota(jnp.int32, (GMLP_CHUNK, GMLP_CHUNK), 0)
    s_idx = lax.broadcasted_iota(jnp.int32, (GMLP_CHUNK, GMLP_CHUNK), 1)
    causal = t_idx >= s_idx
    wm = [jnp.where(causal, ws_ref[hd], 0.0).astype(BF16) for hd in range(nh)]
    rows = []
    for c in range(tm // GMLP_CHUNK):
        r0 = c * GMLP_CHUNK
        cols = []
        for hd in range(nh):
            c0 = hd * GMLP_CHUNK
            s = jnp.dot(wm[hd], vb[r0:r0 + GMLP_CHUNK, c0:c0 + GMLP_CHUNK], preferred_element_type=F32)
            cols.append(s + bs_ref[hd])
        rows.append(jnp.concatenate(cols, axis=1))
    yg = u * jnp.concatenate(rows, axis=0)
    o_ref[...] = _rms(yg, gout_ref[...]).astype(BF16)


def _gmlp(h, gmix, win, gv, ws, bs_full, gout, *, tm):
    m, d = h.shape
    dg = gv.shape[0]
    ublk = (win.shape[1] - 2 * dg) // dg
    kern = functools.partial(_gmlp_kernel, tm=tm, dg=dg)
    return pl.pallas_call(
        kern,
        grid=(m // tm,),
        in_specs=[
            pl.BlockSpec((tm, d), lambda i: (i, 0)),
            _resident((1, d)),
            pl.BlockSpec((d, dg), lambda i: (0, ublk), pipeline_mode=pl.Buffered(1)),
            pl.BlockSpec((d, dg), lambda i: (0, ublk + 1), pipeline_mode=pl.Buffered(1)),
            _resident((1, dg)),
            _resident(ws.shape),
            _resident(bs_full.shape),
            _resident((1, dg)),
        ],
        out_specs=pl.BlockSpec((tm, dg), lambda i: (i, 0)),
        out_shape=jax.ShapeDtypeStruct((m, dg), BF16),
        compiler_params=pltpu.CompilerParams(
            dimension_semantics=("parallel",),
            vmem_limit_bytes=48 << 20,
        ),
        name="gmlp",
    )(h, gmix.reshape(1, d), win, win, gv.reshape(1, dg), ws, bs_full, gout.reshape(1, dg))


def _outproj_kernel(h_ref, ys_ref, yg_ref, wo_ref, o_ref, *, ds):
    acc = jnp.dot(ys_ref[...], wo_ref[:ds, :], preferred_element_type=F32)
    acc += jnp.dot(yg_ref[...], wo_ref[ds:, :], preferred_element_type=F32)
    o_ref[...] = h_ref[...] + acc


def _outproj(h, ys, yg, wo, *, tm):
    m, d = h.shape
    ds = ys.shape[1]
    dg = yg.shape[1]
    kern = functools.partial(_outproj_kernel, ds=ds)
    return pl.pallas_call(
        kern,
        grid=(m // tm,),
        in_specs=[
            pl.BlockSpec((tm, d), lambda i: (i, 0)),
            pl.BlockSpec((tm, ds), lambda i: (i, 0)),
            pl.BlockSpec((tm, dg), lambda i: (i, 0)),
            _resident(wo.shape),
        ],
        out_specs=pl.BlockSpec((tm, d), lambda i: (i, 0)),
        out_shape=jax.ShapeDtypeStruct((m, d), F32),
        compiler_params=pltpu.CompilerParams(
            dimension_semantics=("parallel",),
            vmem_limit_bytes=48 << 20,
        ),
        name="outproj",
    )(h, ys, yg, wo)


def _ple_kernel(h_ref, p_ref, gple_ref, wg_ref, wp_ref, gfin_ref, o_ref):
    h = h_ref[...]
    hn = _rms(h, gple_ref[...]).astype(BF16)
    gate = jax.nn.sigmoid(jnp.dot(hn, wg_ref[...], preferred_element_type=F32))
    pp = jnp.dot(p_ref[...].astype(BF16), wp_ref[...], preferred_element_type=F32)
    o_ref[...] = _rms(h + gate * pp, gfin_ref[...])


def _ple(h, p, gple, wg, wp, gfin, *, tm):
    m, d = h.shape
    dp = p.shape[1]
    return pl.pallas_call(
        _ple_kernel,
        grid=(m // tm,),
        in_specs=[
            pl.BlockSpec((tm, d), lambda i: (i, 0)),
            pl.BlockSpec((tm, dp), lambda i: (i, 0)),
            _resident((1, d)),
            _resident(wg.shape),
            _resident(wp.shape),
            _resident((1, d)),
        ],
        out_specs=pl.BlockSpec((tm, d), lambda i: (i, 0)),
        out_shape=jax.ShapeDtypeStruct((m, d), F32),
        compiler_params=pltpu.CompilerParams(
            dimension_semantics=("parallel",),
            vmem_limit_bytes=48 << 20,
        ),
        name="ple",
    )(h, p, gple.reshape(1, d), wg, wp, gfin.reshape(1, d))


def kernel(x, p, norm_ffn1, w1_gate, w1_up, w1_down, norm_mix, w_in, ssm_log_dt, ssm_a_re, ssm_a_im, ssm_b_re, ssm_b_im, ssm_c_re, ssm_c_im, ssm_d, ssm_w_glu, gmlp_norm_v, gmlp_w_s, gmlp_b_s, norm_ssm_out, norm_gmlp_out, w_out, norm_ffn2, w2_gate, w2_up, w2_down, norm_ple, w_ple_gate, w_ple_proj, norm_final):
    bsz, seqlen, d = x.shape
    depth = p.shape[0]
    m = bsz * seqlen
    dssm = ssm_d.shape[1]
    assert depth == 1, "the per-layer embedding kernel fuses the final norm"
    h = x.reshape(m, d)
    for i in range(depth):
        nrow, nf = m // FFN_TM, w2_gate.shape[2] // FFN_TF
        jobs = [_slab_job(w, nrow * nf, lambda r, c: r * nf + c)
                for w in (w2_gate[i], w2_up[i], w2_down[i], w_in[i], ssm_w_glu[i])]
        h, (w2g, w2u, w2d, w_in_b, wglu_b) = _ffn(
            h, norm_ffn1[i], w1_gate[i].astype(BF16), w1_up[i].astype(BF16), w1_down[i].astype(BF16),
            jobs, tm=FFN_TM, tf=FFN_TF)

        abr, abi, bbr, bbi, cimn = _s5_prep(ssm_log_dt[i], ssm_a_re[i], ssm_a_im[i],
                                            ssm_b_re[i], ssm_b_im[i], ssm_c_im[i])
        bblk = jnp.concatenate([_block_diag_in(bbr), _block_diag_in(bbi)], axis=2).astype(BF16)
        cblk = jnp.concatenate([_block_diag_out(ssm_c_re[i]), _block_diag_out(cimn)], axis=1).astype(BF16)
        nblk = bblk.shape[0]
        ab_shape = (nblk, V7X_SUBLANES, V7X_LANES)
        nsteps = bsz * (seqlen // S5_TC) + 1
        jobs = [_slab_job(w, nsteps) for w in (w_out[i], w_ple_gate[i], w_ple_proj[i])]
        ys, (w_out_b, wpg_b, wpp_b) = _s5(
            h.reshape(bsz, seqlen, d), norm_mix[i], w_in_b, bblk, cblk,
            abr.reshape(ab_shape), abi.reshape(ab_shape), ssm_d[i], wglu_b, norm_ssm_out[i], jobs, tc=S5_TC)

        nh, ck = gmlp_b_s.shape[1:]
        bs_full = jnp.broadcast_to(gmlp_b_s[i][:, :, None], (nh, ck, ck))
        yg = _gmlp(h, norm_mix[i], w_in_b, gmlp_norm_v[i], gmlp_w_s[i], bs_full, norm_gmlp_out[i], tm=ROW_TM)

        h = _outproj(h, ys.reshape(m, dssm), yg, w_out_b, tm=ROW_TM)

        h = _ffn_stream(h, norm_ffn2[i], w2g, w2u, w2d, tm=FFN_TM, tf=FFN_TF)

        h = _ple(h, p[i].reshape(m, -1), norm_ple[i], wpg_b, wpp_b, norm_final, tm=ROW_TM)
    return h.reshape(bsz, seqlen, d)
```

```python
import functools
from typing import Callable, NamedTuple

import jax
import jax.numpy as jnp
from jax import lax
from jax.experimental import pallas as pl
from jax.experimental.pallas import tpu as pltpu

F32 = jnp.float32
BF16 = jnp.bfloat16
EPS = 1e-6

V7X_LANES = 128
V7X_SUBLANES = 8
V7X_VMEM_BYTES = 64 * 1024 * 1024

SSM_BLOCK_GROUPS = 16
GMLP_CHUNK = 128
SCAN_ROW_PITCH = 12

FFN_TM = 1024
FFN_TF = 512
S5_TC = 256
ROW_TM = 512


def _rms(x, g):
    ms = jnp.mean(x * x, axis=-1, keepdims=True)
    return x * lax.rsqrt(ms + EPS) * g


def _resident(shape):
    n = len(shape)
    return pl.BlockSpec(shape, lambda *_: (0,) * n, pipeline_mode=pl.Buffered(1))


class _CastJob(NamedTuple):
    src: jax.Array
    block: tuple
    index_map: Callable


def _cast_specs(jobs):
    specs = [pl.BlockSpec(j.block, j.index_map) for j in jobs]
    shapes = [jax.ShapeDtypeStruct(j.src.shape, BF16) for j in jobs]
    return specs, shapes


def _run_casts(srcs, dsts):
    for src, dst in zip(srcs, dsts):
        dst[...] = src[...].astype(BF16)


BF16_ROWS = 2 * V7X_SUBLANES


def _slab_job(w, nsteps, step_of=lambda s: s):
    r, c = w.shape
    rows = BF16_ROWS
    while r // rows > nsteps:
        rows *= 2
    n = r // rows
    assert r % rows == 0
    return _CastJob(w, (rows, c), lambda *g: (jnp.minimum(step_of(*g), n - 1), 0))


def _ffn_kernel(*refs, n_cast):
    n = n_cast
    x_ref, g_ref, wg_ref, wu_ref, wd_ref = refs[:5]
    cast_src, o_ref, cast_dst, xn_ref = refs[5:5 + n], refs[5 + n], refs[6 + n:6 + 2 * n], refs[6 + 2 * n]

    @pl.when(pl.program_id(1) == 0)
    def _():
        x = x_ref[...]
        xn_ref[...] = _rms(x, g_ref[...]).astype(BF16)
        o_ref[...] = x

    _run_casts(cast_src, cast_dst)
    xn = xn_ref[...]
    gate = jnp.dot(xn, wg_ref[...], preferred_element_type=F32)
    up = jnp.dot(xn, wu_ref[...], preferred_element_type=F32)
    act = (gate * jax.nn.sigmoid(gate) * (0.5 * up)).astype(BF16)
    o_ref[...] += jnp.dot(act, wd_ref[...], preferred_element_type=F32)


def _ffn(x, g, wg, wu, wd, casts=(), *, tm, tf):
    m, d = x.shape
    f = wg.shape[1]
    row = pl.BlockSpec((tm, d), lambda i, j: (i, 0))
    cast_specs, cast_shapes = _cast_specs(casts)
    vmem = (2 * tm * d * 4) * 2 + tm * d * 2 + 3 * 2 * d * tf * 2 + 4 * tm * tf * 4
    vmem += sum(c.block[0] * c.block[1] * 2 * (4 + 2) for c in casts)
    outs = pl.pallas_call(
        functools.partial(_ffn_kernel, n_cast=len(casts)),
        grid=(m // tm, f // tf),
        in_specs=[
            row,
            pl.BlockSpec((1, d), lambda i, j: (0, 0)),
            pl.BlockSpec((d, tf), lambda i, j: (0, j)),
            pl.BlockSpec((d, tf), lambda i, j: (0, j)),
            pl.BlockSpec((tf, d), lambda i, j: (j, 0)),
        ] + cast_specs,
        out_specs=[row] + cast_specs,
        out_shape=[jax.ShapeDtypeStruct((m, d), F32)] + cast_shapes,
        scratch_shapes=[pltpu.VMEM((tm, d), BF16)],
        compiler_params=pltpu.CompilerParams(
            dimension_semantics=("parallel", "arbitrary"),
            vmem_limit_bytes=min(vmem + (8 << 20), V7X_VMEM_BYTES - (2 << 20)),
        ),
        name="ffn",
    )(x, g.reshape(1, d), wg, wu, wd, *[c.src for c in casts])
    return outs[0], list(outs[1:])


def _zoh(logdt, a_re, a_im):
    dt = jnp.exp(logdt)
    lr = jnp.minimum(a_re, -1e-4)
    li = a_im
    mag = jnp.exp(lr * dt)
    ang = li * dt
    abr = mag * jnp.cos(ang)
    abi = mag * jnp.sin(ang)
    den = lr * lr + li * li
    xr = abr - 1.0
    xi = abi
    zr = (xr * lr + xi * li) / den
    zi = (xi * lr - xr * li) / den
    return abr, abi, zr, zi


def _s5_build_params(ldt_ref, are_ref, aim_ref, btre_ref, btim_ref, cre_ref, cim_ref,
                     bblk_ref, cblk_ref, ct_ref, abr_ref, abi_ref):
    ngroup, p, n = btre_ref.shape
    bg = SSM_BLOCK_GROUPS
    abr, abi, zr, zi = _zoh(ldt_ref[...], are_ref[...], aim_ref[...])
    abr_ref[...] = abr.reshape(abr_ref.shape)
    abi_ref[...] = abi.reshape(abi_ref.shape)
    per_row = V7X_LANES // n
    zero = jnp.zeros((p, n), F32)

    def place(piece, gi):
        return jnp.concatenate([piece if s == gi % per_row else zero for s in range(per_row)], axis=1)

    bblk_ref[...] = jnp.zeros_like(bblk_ref)
    for k in range(ngroup // bg):
        ct_ref[...] = jnp.zeros_like(ct_ref)
        for gi in range(bg):
            g = k * bg + gi
            row, lo = g // per_row, (g % per_row) * n
            zr_g = zr[row:row + 1, lo:lo + n]
            zi_g = zi[row:row + 1, lo:lo + n]
            br = btre_ref[g]
            bi = btim_ref[g]
            rows = slice(gi * p, (gi + 1) * p)
            re = slice((gi // per_row) * V7X_LANES, (gi // per_row + 1) * V7X_LANES)
            im = slice(bg * n + re.start, bg * n + re.stop)
            bblk_ref[k, rows, re] = place(zr_g * br - zi_g * bi, gi).astype(BF16)
            bblk_ref[k, rows, im] = place(zr_g * bi + zi_g * br, gi).astype(BF16)
            ct_ref[rows, re] = place(cre_ref[g], gi)
            ct_ref[rows, im] = place(-cim_ref[g], gi)
        cblk_ref[k] = ct_ref[...].T.astype(BF16)


def _s5_kernel(*refs, tc, nblk, bch, bst, chunks_per_seq, n_cast):
    h_ref, gmix_ref, win_ref, d_ref, wglu_ref, gout_ref = refs[:6]
    raw_params = refs[6:13]
    cast_src, o_ref, cast_dst = refs[13:13 + n_cast], refs[13 + n_cast], refs[14 + n_cast:14 + 2 * n_cast]
    (dre0_ref, dim0_ref, u0_ref, dre1_ref, dim1_ref, u1_ref, st_ref,
     bblk_ref, cblk_ref, ct_ref, abr_ref, abi_ref) = refs[14 + 2 * n_cast:]
    ntile = bst // V7X_LANES
    nb = tc // V7X_SUBLANES
    step_id = pl.program_id(0)
    bufs = ((dre0_ref, dim0_ref, u0_ref), (dre1_ref, dim1_ref, u1_ref))
    _run_casts(cast_src, cast_dst)

    @pl.when(step_id == 0)
    def _():
        _s5_build_params(*raw_params, bblk_ref, cblk_ref, ct_ref, abr_ref, abi_ref)
        st_ref[...] = jnp.zeros_like(st_ref)
        for ref in bufs[1]:
            ref[...] = jnp.zeros_like(ref)

    def step(wr, rd):
        dre_w, dim_w, u_w = wr
        dre_r, dim_r, u_r = rd

        keep = jnp.where((step_id - 1) % chunks_per_seq == 0, 0.0, 1.0).astype(F32)
        ar = [abr_ref[k] for k in range(nblk)]
        ai = [abi_ref[k] for k in range(nblk)]
        s = [st_ref[i] * keep for i in range(2 * nblk)]
        for tb in range(nb):
            for r in range(V7X_SUBLANES):
                rows = slice(r * SCAN_ROW_PITCH, r * SCAN_ROW_PITCH + ntile)
                for k in range(nblk):
                    sre, sim = s[2 * k], s[2 * k + 1]
                    nre = ar[k] * sre - ai[k] * sim + dre_r[k, tb, rows, :]
                    nim = ar[k] * sim + ai[k] * sre + dim_r[k, tb, rows, :]
                    dre_r[k, tb, rows, :] = nre
                    dim_r[k, tb, rows, :] = nim
                    s[2 * k], s[2 * k + 1] = nre, nim
        for i in range(2 * nblk):
            st_ref[i] = s[i]

        hn = _rms(h_ref[0], gmix_ref[...]).astype(BF16)
        u = jnp.dot(hn, win_ref[...], preferred_element_type=F32)
        u_w[...] = u
        ub = u.astype(BF16)
        for k in range(nblk):
            drv = jnp.dot(ub[:, k * bch:(k + 1) * bch], bblk_ref[k], preferred_element_type=F32)
            for j in range(ntile):
                lo = j * V7X_LANES
                tile = pl.ds(j, V7X_SUBLANES, stride=SCAN_ROW_PITCH)
                dre_w[k, :, tile, :] = drv[:, lo:lo + V7X_LANES].reshape(nb, V7X_SUBLANES, V7X_LANES)
                dim_w[k, :, tile, :] = drv[:, bst + lo:bst + lo + V7X_LANES].reshape(nb, V7X_SUBLANES, V7X_LANES)

        ys = []
        for k in range(nblk):
            parts = []
            for ref in (dre_r, dim_r):
                for j in range(ntile):
                    tile = pl.ds(j, V7X_SUBLANES, stride=SCAN_ROW_PITCH)
                    parts.append(ref[k, :, tile, :].reshape(tc, V7X_LANES))
            lhs = jnp.concatenate(parts, axis=1).astype(BF16)
            ys.append(jnp.dot(lhs, cblk_ref[k], preferred_element_type=F32))
        y = jnp.concatenate(ys, axis=1) + d_ref[...] * u_r[...]
        y = jax.nn.gelu(y)
        y = y * jax.nn.sigmoid(jnp.dot(y.astype(BF16), wglu_ref[...], preferred_element_type=F32))
        o_ref[0] = _rms(y, gout_ref[...]).astype(BF16)

    @pl.when(step_id % 2 == 0)
    def _():
        step(bufs[0], bufs[1])

    @pl.when(step_id % 2 == 1)
    def _():
        step(bufs[1], bufs[0])


def _s5(h3, gmix, win, log_dt, a_re, a_im, b_re, b_im, c_re, c_im, dskip, wglu, gout, casts=(), *, tc):
    b, l, d = h3.shape
    dssm = dskip.shape[0]
    ngroup, nstate, p = b_re.shape
    nblk, bch, bst = ngroup // SSM_BLOCK_GROUPS, SSM_BLOCK_GROUPS * p, SSM_BLOCK_GROUPS * nstate
    ntile = bst // V7X_LANES
    assert ntile == V7X_SUBLANES and tc % V7X_SUBLANES == 0 and V7X_LANES % nstate == 0
    nb = tc // V7X_SUBLANES
    flat = (ngroup * nstate // V7X_LANES, V7X_LANES)
    raw = [jnp.repeat(log_dt, nstate).reshape(flat), a_re.reshape(flat), a_im.reshape(flat),
           jnp.swapaxes(b_re, 1, 2), jnp.swapaxes(b_im, 1, 2), c_re, c_im]
    nch = l // tc
    last = b * nch - 1
    kern = functools.partial(_s5_kernel, tc=tc, nblk=nblk, bch=bch, bst=bst, chunks_per_seq=nch,
                             n_cast=len(casts))
    cast_specs, cast_shapes = _cast_specs(casts)
    dbuf = pltpu.VMEM((nblk, nb, V7X_SUBLANES * SCAN_ROW_PITCH, V7X_LANES), F32)
    ubuf = pltpu.VMEM((tc, dssm), F32)

    def in_chunk(s):
        c = jnp.minimum(s, last)
        return (c // nch, c % nch, 0)

    def out_chunk(s):
        c = jnp.maximum(s - 1, 0)
        return (c // nch, c % nch, 0)

    outs = pl.pallas_call(
        kern,
        grid=(b * nch + 1,),
        in_specs=[
            pl.BlockSpec((1, tc, d), in_chunk),
            _resident((1, d)),
            pl.BlockSpec((d, dssm), lambda s: (0, 0), pipeline_mode=pl.Buffered(1)),
            _resident((1, dssm)),
            _resident(wglu.shape),
            _resident((1, dssm)),
        ] + [_resident(a.shape) for a in raw] + cast_specs,
        out_specs=[pl.BlockSpec((1, tc, dssm), out_chunk)] + cast_specs,
        out_shape=[jax.ShapeDtypeStruct((b, l, dssm), BF16)] + cast_shapes,
        scratch_shapes=[
            dbuf, dbuf, ubuf, dbuf, dbuf, ubuf,
            pltpu.VMEM((2 * nblk, V7X_SUBLANES, V7X_LANES), F32),
            pltpu.VMEM((nblk, bch, 2 * bst), BF16),
            pltpu.VMEM((nblk, 2 * bst, bch), BF16),
            pltpu.VMEM((bch, 2 * bst), F32),
            pltpu.VMEM((nblk, V7X_SUBLANES, V7X_LANES), F32),
            pltpu.VMEM((nblk, V7X_SUBLANES, V7X_LANES), F32),
        ],
        compiler_params=pltpu.CompilerParams(
            dimension_semantics=("arbitrary",),
            vmem_limit_bytes=56 << 20,
        ),
        name="s5",
    )(h3, gmix.reshape(1, d), win, dskip.reshape(1, dssm), wglu, gout.reshape(1, dssm), *raw,
      *[c.src for c in casts])
    return outs[0], list(outs[1:])


def _gmlp_kernel(h_ref, gmix_ref, wu_ref, wv_ref, gv_ref, ws_ref, bs_ref, gout_ref, o_ref, *, tm, dg):
    hn = _rms(h_ref[...], gmix_ref[...]).astype(BF16)
    zu = jnp.dot(hn, wu_ref[...], preferred_element_type=F32)
    zv = jnp.dot(hn, wv_ref[...], preferred_element_type=F32)
    u = jax.nn.gelu(zu)
    v = jax.nn.gelu(zv)
    vc = v - jnp.mean(v, axis=-1, keepdims=True)
    vn = vc * lax.rsqrt(jnp.mean(vc * vc, axis=-1, keepdims=True) + EPS) * gv_ref[...]
    vb = vn.astype(BF16)

    nh = dg // GMLP_CHUNK
    t_idx = lax.broadcasted_iota(jnp.int32, (GMLP_CHUNK, GMLP_CHUNK), 0)
    s_idx = lax.broadcasted_iota(jnp.int32, (GMLP_CHUNK, GMLP_CHUNK), 1)
    causal = t_idx >= s_idx
    nchunk = tm // GMLP_CHUNK
    cols = []
    for hd in range(nh):
        c0 = hd * GMLP_CHUNK
        wm = jnp.where(causal, ws_ref[hd], 0.0).astype(BF16)
        v_h = jnp.concatenate([vb[c * GMLP_CHUNK:(c + 1) * GMLP_CHUNK, c0:c0 + GMLP_CHUNK]
                               for c in range(nchunk)], axis=1)
        s_h = jnp.dot(wm, v_h, preferred_element_type=F32)
        cols.append(jnp.concatenate([s_h[:, c * GMLP_CHUNK:(c + 1) * GMLP_CHUNK] + bs_ref[hd]
                                     for c in range(nchunk)], axis=0))
    yg = u * jnp.concatenate(cols, axis=1)
    o_ref[...] = _rms(yg, gout_ref[...]).astype(BF16)


def _gmlp(h, gmix, win, gv, ws, bs_full, gout, *, tm):
    m, d = h.shape
    dg = gv.shape[0]
    ublk = (win.shape[1] - 2 * dg) // dg
    kern = functools.partial(_gmlp_kernel, tm=tm, dg=dg)
    return pl.pallas_call(
        kern,
        grid=(m // tm,),
        in_specs=[
            pl.BlockSpec((tm, d), lambda i: (i, 0)),
            _resident((1, d)),
            pl.BlockSpec((d, dg), lambda i: (0, ublk), pipeline_mode=pl.Buffered(1)),
            pl.BlockSpec((d, dg), lambda i: (0, ublk + 1), pipeline_mode=pl.Buffered(1)),
            _resident((1, dg)),
            _resident(ws.shape),
            _resident(bs_full.shape),
            _resident((1, dg)),
        ],
        out_specs=pl.BlockSpec((tm, dg), lambda i: (i, 0)),
        out_shape=jax.ShapeDtypeStruct((m, dg), BF16),
        compiler_params=pltpu.CompilerParams(
            dimension_semantics=("parallel",),
            vmem_limit_bytes=48 << 20,
        ),
        name="gmlp",
    )(h, gmix.reshape(1, d), win, win, gv.reshape(1, dg), ws, bs_full, gout.reshape(1, dg))


def _outproj_kernel(h_ref, ys_ref, yg_ref, wo_ref, o_ref, *, ds):
    acc = jnp.dot(ys_ref[...], wo_ref[:ds, :], preferred_element_type=F32)
    acc += jnp.dot(yg_ref[...], wo_ref[ds:, :], preferred_element_type=F32)
    o_ref[...] = h_ref[...] + acc


def _outproj(h, ys, yg, wo, *, tm):
    m, d = h.shape
    ds = ys.shape[1]
    dg = yg.shape[1]
    kern = functools.partial(_outproj_kernel, ds=ds)
    return pl.pallas_call(
        kern,
        grid=(m // tm,),
        in_specs=[
            pl.BlockSpec((tm, d), lambda i: (i, 0)),
            pl.BlockSpec((tm, ds), lambda i: (i, 0)),
            pl.BlockSpec((tm, dg), lambda i: (i, 0)),
            _resident(wo.shape),
        ],
        out_specs=pl.BlockSpec((tm, d), lambda i: (i, 0)),
        out_shape=jax.ShapeDtypeStruct((m, d), F32),
        compiler_params=pltpu.CompilerParams(
            dimension_semantics=("parallel",),
            vmem_limit_bytes=48 << 20,
        ),
        name="outproj",
    )(h, ys, yg, wo)


def _ple_kernel(h_ref, p_ref, gple_ref, wg_ref, wp_ref, gfin_ref, o_ref):
    h = h_ref[...]
    hn = _rms(h, gple_ref[...]).astype(BF16)
    gate = jax.nn.sigmoid(jnp.dot(hn, wg_ref[...], preferred_element_type=F32))
    pp = jnp.dot(p_ref[...].astype(BF16), wp_ref[...], preferred_element_type=F32)
    o_ref[...] = _rms(h + gate * pp, gfin_ref[...])


def _ple(h, p, gple, wg, wp, gfin, *, tm):
    m, d = h.shape
    dp = p.shape[1]
    return pl.pallas_call(
        _ple_kernel,
        grid=(m // tm,),
        in_specs=[
            pl.BlockSpec((tm, d), lambda i: (i, 0)),
            pl.BlockSpec((tm, dp), lambda i: (i, 0)),
            _resident((1, d)),
            _resident(wg.shape),
            _resident(wp.shape),
            _resident((1, d)),
        ],
        out_specs=pl.BlockSpec((tm, d), lambda i: (i, 0)),
        out_shape=jax.ShapeDtypeStruct((m, d), F32),
        compiler_params=pltpu.CompilerParams(
            dimension_semantics=("parallel",),
            vmem_limit_bytes=48 << 20,
        ),
        name="ple",
    )(h, p, gple.reshape(1, d), wg, wp, gfin.reshape(1, d))


def kernel(x, p, norm_ffn1, w1_gate, w1_up, w1_down, norm_mix, w_in, ssm_log_dt, ssm_a_re, ssm_a_im, ssm_b_re, ssm_b_im, ssm_c_re, ssm_c_im, ssm_d, ssm_w_glu, gmlp_norm_v, gmlp_w_s, gmlp_b_s, norm_ssm_out, norm_gmlp_out, w_out, norm_ffn2, w2_gate, w2_up, w2_down, norm_ple, w_ple_gate, w_ple_proj, norm_final):
    bsz, seqlen, d = x.shape
    depth = p.shape[0]
    m = bsz * seqlen
    dssm = ssm_d.shape[1]
    assert depth == 1, "the per-layer embedding kernel fuses the final norm"
    h = x.reshape(m, d)
    for i in range(depth):
        nrow, nf = m // FFN_TM, w2_gate.shape[2] // FFN_TF
        jobs = [_slab_job(w, nrow * nf, lambda r, c: r * nf + c)
                for w in (w2_gate[i], w2_up[i], w2_down[i], w_in[i], ssm_w_glu[i])]
        h, (w2g, w2u, w2d, w_in_b, wglu_b) = _ffn(
            h, norm_ffn1[i], w1_gate[i].astype(BF16), w1_up[i].astype(BF16), w1_down[i].astype(BF16),
            jobs, tm=FFN_TM, tf=FFN_TF)

        nsteps = bsz * (seqlen // S5_TC) + 1
        jobs = [_slab_job(w, nsteps) for w in (w_out[i], w_ple_gate[i], w_ple_proj[i])]
        ys, (w_out_b, wpg_b, wpp_b) = _s5(
            h.reshape(bsz, seqlen, d), norm_mix[i], w_in_b,
            ssm_log_dt[i], ssm_a_re[i], ssm_a_im[i], ssm_b_re[i], ssm_b_im[i], ssm_c_re[i], ssm_c_im[i],
            ssm_d[i], wglu_b, norm_ssm_out[i], jobs, tc=S5_TC)

        nh, ck = gmlp_b_s.shape[1:]
        bs_full = jnp.broadcast_to(gmlp_b_s[i][:, :, None], (nh, ck, ck))
        yg = _gmlp(h, norm_mix[i], w_in_b, gmlp_norm_v[i], gmlp_w_s[i], bs_full, norm_gmlp_out[i], tm=ROW_TM)

        h = _outproj(h, ys.reshape(m, dssm), yg, w_out_b, tm=ROW_TM)

        h, _ = _ffn(h, norm_ffn2[i], w2g, w2u, w2d, tm=FFN_TM, tf=FFN_TF)

        h = _ple(h, p[i].reshape(m, -1), norm_ple[i], wpg_b, wpp_b, norm_final, tm=ROW_TM)
    return h.reshape(bsz, seqlen, d)
```

```python
import functools
from typing import Callable, NamedTuple

import jax
import jax.numpy as jnp
from jax import lax
from jax.experimental import pallas as pl
from jax.experimental.pallas import tpu as pltpu

F32 = jnp.float32
BF16 = jnp.bfloat16
EPS = 1e-6

V7X_LANES = 128
V7X_SUBLANES = 8
V7X_VMEM_BYTES = 64 * 1024 * 1024

SSM_BLOCK_GROUPS = 16
GMLP_CHUNK = 128
SCAN_ROW_PITCH = 12

FFN_TM = 1024
FFN_TF = 512
FFN_HEAD_TILES = 3
FFN_HEAD_TF = 256
S5_TC = 256
ROW_TM = 512


def _rms(x, g):
    ms = jnp.mean(x * x, axis=-1, keepdims=True)
    return x * lax.rsqrt(ms + EPS) * g


def _resident(shape):
    n = len(shape)
    return pl.BlockSpec(shape, lambda *_: (0,) * n, pipeline_mode=pl.Buffered(1))


class _CastJob(NamedTuple):
    src: jax.Array
    block: tuple
    index_map: Callable


def _cast_specs(jobs):
    specs = [pl.BlockSpec(j.block, j.index_map) for j in jobs]
    shapes = [jax.ShapeDtypeStruct(j.src.shape, BF16) for j in jobs]
    return specs, shapes


def _run_casts(srcs, dsts):
    for src, dst in zip(srcs, dsts):
        dst[...] = src[...].astype(BF16)


BF16_ROWS = 2 * V7X_SUBLANES


def _slab_job(w, nsteps, step_of=lambda s: s):
    r, c = w.shape
    rows = BF16_ROWS
    while r // rows > nsteps:
        rows *= 2
    n = r // rows
    assert r % rows == 0
    return _CastJob(w, (rows, c), lambda *g: (jnp.minimum(step_of(*g), n - 1), 0))


def _ffn_kernel(*refs, n_cast, has_base):
    n = n_cast
    x_ref, g_ref, wg_ref, wu_ref, wd_ref = refs[:5]
    cast_src = refs[5:5 + n]
    outs = refs[5 + n + int(has_base):]
    o_ref, cast_dst, xn_ref = outs[0], outs[1:1 + n], outs[1 + n]

    @pl.when(pl.program_id(1) == 0)
    def _():
        x = x_ref[...]
        xn_ref[...] = _rms(x, g_ref[...]).astype(BF16)
        o_ref[...] = x

    _run_casts(cast_src, cast_dst)
    xn = xn_ref[...]
    gate = jnp.dot(xn, wg_ref[...].astype(BF16), preferred_element_type=F32)
    up = jnp.dot(xn, wu_ref[...].astype(BF16), preferred_element_type=F32)
    act = (gate * jax.nn.sigmoid(gate) * (0.5 * up)).astype(BF16)
    o_ref[...] += jnp.dot(act, wd_ref[...].astype(BF16), preferred_element_type=F32)


def _ffn(x, g, wg, wu, wd, casts=(), *, tm, tf, tiles=None, base=None):
    m, d = x.shape
    f = wg.shape[1]
    first, ntiles = tiles if tiles is not None else (0, m // tm)
    row = pl.BlockSpec((tm, d), lambda i, j: (i + first, 0))
    cast_specs, cast_shapes = _cast_specs(casts)
    wbytes = wg.dtype.itemsize
    vmem = (2 * tm * d * 4) * 2 + tm * d * 2 + 3 * 2 * d * tf * wbytes + 4 * tm * tf * 4
    vmem += sum(c.block[0] * c.block[1] * 2 * (4 + 2) for c in casts)
    n_in = 5 + len(casts)
    outs = pl.pallas_call(
        functools.partial(_ffn_kernel, n_cast=len(casts), has_base=base is not None),
        grid=(ntiles, f // tf),
        in_specs=[
            row,
            pl.BlockSpec((1, d), lambda i, j: (0, 0)),
            pl.BlockSpec((d, tf), lambda i, j: (0, j)),
            pl.BlockSpec((d, tf), lambda i, j: (0, j)),
            pl.BlockSpec((tf, d), lambda i, j: (j, 0)),
        ] + cast_specs + ([pl.BlockSpec(memory_space=pl.ANY)] if base is not None else []),
        out_specs=[row] + cast_specs,
        out_shape=[jax.ShapeDtypeStruct((m, d), F32)] + cast_shapes,
        input_output_aliases={n_in: 0} if base is not None else {},
        scratch_shapes=[pltpu.VMEM((tm, d), BF16)],
        compiler_params=pltpu.CompilerParams(
            dimension_semantics=("parallel", "arbitrary"),
            vmem_limit_bytes=min(vmem + (8 << 20), V7X_VMEM_BYTES - (2 << 20)),
        ),
        name="ffn",
    )(x, g.reshape(1, d), wg, wu, wd, *[c.src for c in casts], *([base] if base is not None else []))
    return outs[0], list(outs[1:])


def _zoh(logdt, a_re, a_im):
    dt = jnp.exp(logdt)
    lr = jnp.minimum(a_re, -1e-4)
    li = a_im
    mag = jnp.exp(lr * dt)
    ang = li * dt
    abr = mag * jnp.cos(ang)
    abi = mag * jnp.sin(ang)
    den = lr * lr + li * li
    xr = abr - 1.0
    xi = abi
    zr = (xr * lr + xi * li) / den
    zi = (xi * lr - xr * li) / den
    return abr, abi, zr, zi


def _s5_build_params(ldt_ref, are_ref, aim_ref, btre_ref, btim_ref, cre_ref, cim_ref,
                     bblk_ref, cblk_ref, ct_ref, abr_ref, abi_ref):
    ngroup, p, n = btre_ref.shape
    bg = SSM_BLOCK_GROUPS
    abr, abi, zr, zi = _zoh(ldt_ref[...], are_ref[...], aim_ref[...])
    abr_ref[...] = abr.reshape(abr_ref.shape)
    abi_ref[...] = abi.reshape(abi_ref.shape)
    per_row = V7X_LANES // n
    zero = jnp.zeros((p, n), F32)

    def place(piece, gi):
        return jnp.concatenate([piece if s == gi % per_row else zero for s in range(per_row)], axis=1)

    bblk_ref[...] = jnp.zeros_like(bblk_ref)
    for k in range(ngroup // bg):
        ct_ref[...] = jnp.zeros_like(ct_ref)
        for gi in range(bg):
            g = k * bg + gi
            row, lo = g // per_row, (g % per_row) * n
            zr_g = zr[row:row + 1, lo:lo + n]
            zi_g = zi[row:row + 1, lo:lo + n]
            br = btre_ref[g]
            bi = btim_ref[g]
            rows = slice(gi * p, (gi + 1) * p)
            re = slice((gi // per_row) * V7X_LANES, (gi // per_row + 1) * V7X_LANES)
            im = slice(bg * n + re.start, bg * n + re.stop)
            bblk_ref[k, rows, re] = place(zr_g * br - zi_g * bi, gi).astype(BF16)
            bblk_ref[k, rows, im] = place(zr_g * bi + zi_g * br, gi).astype(BF16)
            ct_ref[rows, re] = place(cre_ref[g], gi)
            ct_ref[rows, im] = place(-cim_ref[g], gi)
        cblk_ref[k] = ct_ref[...].T.astype(BF16)


def _s5_kernel(*refs, tc, nblk, bch, bst, chunks_per_seq, n_cast):
    h_ref, gmix_ref, win_ref, d_ref, wglu_ref, gout_ref = refs[:6]
    raw_params = refs[6:13]
    cast_src, o_ref, cast_dst = refs[13:13 + n_cast], refs[13 + n_cast], refs[14 + n_cast:14 + 2 * n_cast]
    (dre0_ref, dim0_ref, u0_ref, dre1_ref, dim1_ref, u1_ref, st_ref,
     bblk_ref, cblk_ref, ct_ref, abr_ref, abi_ref) = refs[14 + 2 * n_cast:]
    ntile = bst // V7X_LANES
    nb = tc // V7X_SUBLANES
    step_id = pl.program_id(0)
    bufs = ((dre0_ref, dim0_ref, u0_ref), (dre1_ref, dim1_ref, u1_ref))
    _run_casts(cast_src, cast_dst)

    @pl.when(step_id == 0)
    def _():
        _s5_build_params(*raw_params, bblk_ref, cblk_ref, ct_ref, abr_ref, abi_ref)
        st_ref[...] = jnp.zeros_like(st_ref)
        for ref in bufs[1]:
            ref[...] = jnp.zeros_like(ref)

    def step(wr, rd):
        dre_w, dim_w, u_w = wr
        dre_r, dim_r, u_r = rd

        keep = jnp.where((step_id - 1) % chunks_per_seq == 0, 0.0, 1.0).astype(F32)
        ar = [abr_ref[k] for k in range(nblk)]
        ai = [abi_ref[k] for k in range(nblk)]
        s = [st_ref[i] * keep for i in range(2 * nblk)]
        for tb in range(nb):
            for r in range(V7X_SUBLANES):
                rows = slice(r * SCAN_ROW_PITCH, r * SCAN_ROW_PITCH + ntile)
                for k in range(nblk):
                    sre, sim = s[2 * k], s[2 * k + 1]
                    nre = ar[k] * sre - ai[k] * sim + dre_r[k, tb, rows, :]
                    nim = ar[k] * sim + ai[k] * sre + dim_r[k, tb, rows, :]
                    dre_r[k, tb, rows, :] = nre
                    dim_r[k, tb, rows, :] = nim
                    s[2 * k], s[2 * k + 1] = nre, nim
        for i in range(2 * nblk):
            st_ref[i] = s[i]

        hn = _rms(h_ref[0], gmix_ref[...]).astype(BF16)
        u = jnp.dot(hn, win_ref[...], preferred_element_type=F32)
        u_w[...] = u
        ub = u.astype(BF16)
        for k in range(nblk):
            drv = jnp.dot(ub[:, k * bch:(k + 1) * bch], bblk_ref[k], preferred_element_type=F32)
            for j in range(ntile):
                lo = j * V7X_LANES
                tile = pl.ds(j, V7X_SUBLANES, stride=SCAN_ROW_PITCH)
                dre_w[k, :, tile, :] = drv[:, lo:lo + V7X_LANES].reshape(nb, V7X_SUBLANES, V7X_LANES)
                dim_w[k, :, tile, :] = drv[:, bst + lo:bst + lo + V7X_LANES].reshape(nb, V7X_SUBLANES, V7X_LANES)

        ys = []
        for k in range(nblk):
            parts = []
            for ref in (dre_r, dim_r):
                for j in range(ntile):
                    tile = pl.ds(j, V7X_SUBLANES, stride=SCAN_ROW_PITCH)
                    parts.append(ref[k, :, tile, :].reshape(tc, V7X_LANES))
            lhs = jnp.concatenate(parts, axis=1).astype(BF16)
            ys.append(jnp.dot(lhs, cblk_ref[k], preferred_element_type=F32))
        y = jnp.concatenate(ys, axis=1) + d_ref[...] * u_r[...]
        y = jax.nn.gelu(y)
        y = y * jax.nn.sigmoid(jnp.dot(y.astype(BF16), wglu_ref[...], preferred_element_type=F32))
        o_ref[0] = _rms(y, gout_ref[...]).astype(BF16)

    @pl.when(step_id % 2 == 0)
    def _():
        step(bufs[0], bufs[1])

    @pl.when(step_id % 2 == 1)
    def _():
        step(bufs[1], bufs[0])


def _s5(h3, gmix, win, log_dt, a_re, a_im, b_re, b_im, c_re, c_im, dskip, wglu, gout, casts=(), *, tc):
    b, l, d = h3.shape
    dssm = dskip.shape[0]
    ngroup, nstate, p = b_re.shape
    nblk, bch, bst = ngroup // SSM_BLOCK_GROUPS, SSM_BLOCK_GROUPS * p, SSM_BLOCK_GROUPS * nstate
    ntile = bst // V7X_LANES
    assert ntile == V7X_SUBLANES and tc % V7X_SUBLANES == 0 and V7X_LANES % nstate == 0
    nb = tc // V7X_SUBLANES
    flat = (ngroup * nstate // V7X_LANES, V7X_LANES)
    raw = [jnp.repeat(log_dt, nstate).reshape(flat), a_re.reshape(flat), a_im.reshape(flat),
           jnp.swapaxes(b_re, 1, 2), jnp.swapaxes(b_im, 1, 2), c_re, c_im]
    nch = l // tc
    last = b * nch - 1
    kern = functools.partial(_s5_kernel, tc=tc, nblk=nblk, bch=bch, bst=bst, chunks_per_seq=nch,
                             n_cast=len(casts))
    cast_specs, cast_shapes = _cast_specs(casts)
    dbuf = pltpu.VMEM((nblk, nb, V7X_SUBLANES * SCAN_ROW_PITCH, V7X_LANES), F32)
    ubuf = pltpu.VMEM((tc, dssm), F32)

    def in_chunk(s):
        c = jnp.minimum(s, last)
        return (c // nch, c % nch, 0)

    def out_chunk(s):
        c = jnp.maximum(s - 1, 0)
        return (c // nch, c % nch, 0)

    outs = pl.pallas_call(
        kern,
        grid=(b * nch + 1,),
        in_specs=[
            pl.BlockSpec((1, tc, d), in_chunk),
            _resident((1, d)),
            pl.BlockSpec((d, dssm), lambda s: (0, 0), pipeline_mode=pl.Buffered(1)),
            _resident((1, dssm)),
            _resident(wglu.shape),
            _resident((1, dssm)),
        ] + [_resident(a.shape) for a in raw] + cast_specs,
        out_specs=[pl.BlockSpec((1, tc, dssm), out_chunk)] + cast_specs,
        out_shape=[jax.ShapeDtypeStruct((b, l, dssm), BF16)] + cast_shapes,
        scratch_shapes=[
            dbuf, dbuf, ubuf, dbuf, dbuf, ubuf,
            pltpu.VMEM((2 * nblk, V7X_SUBLANES, V7X_LANES), F32),
            pltpu.VMEM((nblk, bch, 2 * bst), BF16),
            pltpu.VMEM((nblk, 2 * bst, bch), BF16),
            pltpu.VMEM((bch, 2 * bst), F32),
            pltpu.VMEM((nblk, V7X_SUBLANES, V7X_LANES), F32),
            pltpu.VMEM((nblk, V7X_SUBLANES, V7X_LANES), F32),
        ],
        compiler_params=pltpu.CompilerParams(
            dimension_semantics=("arbitrary",),
            vmem_limit_bytes=56 << 20,
        ),
        name="s5",
    )(h3, gmix.reshape(1, d), win, dskip.reshape(1, dssm), wglu, gout.reshape(1, dssm), *raw,
      *[c.src for c in casts])
    return outs[0], list(outs[1:])


def _gmlp_kernel(h_ref, gmix_ref, wu_ref, wv_ref, gv_ref, ws_ref, bs_ref, gout_ref, o_ref, *, tm, dg):
    hn = _rms(h_ref[...], gmix_ref[...]).astype(BF16)
    zu = jnp.dot(hn, wu_ref[...], preferred_element_type=F32)
    zv = jnp.dot(hn, wv_ref[...], preferred_element_type=F32)
    u = jax.nn.gelu(zu)
    v = jax.nn.gelu(zv)
    vc = v - jnp.mean(v, axis=-1, keepdims=True)
    vn = vc * lax.rsqrt(jnp.mean(vc * vc, axis=-1, keepdims=True) + EPS) * gv_ref[...]
    vb = vn.astype(BF16)

    nh = dg // GMLP_CHUNK
    t_idx = lax.broadcasted_iota(jnp.int32, (GMLP_CHUNK, GMLP_CHUNK), 0)
    s_idx = lax.broadcasted_iota(jnp.int32, (GMLP_CHUNK, GMLP_CHUNK), 1)
    causal = t_idx >= s_idx
    nchunk = tm // GMLP_CHUNK
    cols = []
    for hd in range(nh):
        c0 = hd * GMLP_CHUNK
        wm = jnp.where(causal, ws_ref[hd], 0.0).astype(BF16)
        v_h = jnp.concatenate([vb[c * GMLP_CHUNK:(c + 1) * GMLP_CHUNK, c0:c0 + GMLP_CHUNK]
                               for c in range(nchunk)], axis=1)
        s_h = jnp.dot(wm, v_h, preferred_element_type=F32)
        cols.append(jnp.concatenate([s_h[:, c * GMLP_CHUNK:(c + 1) * GMLP_CHUNK] + bs_ref[hd]
                                     for c in range(nchunk)], axis=0))
    yg = u * jnp.concatenate(cols, axis=1)
    o_ref[...] = _rms(yg, gout_ref[...]).astype(BF16)


def _gmlp(h, gmix, win, gv, ws, bs_full, gout, *, tm):
    m, d = h.shape
    dg = gv.shape[0]
    ublk = (win.shape[1] - 2 * dg) // dg
    kern = functools.partial(_gmlp_kernel, tm=tm, dg=dg)
    return pl.pallas_call(
        kern,
        grid=(m // tm,),
        in_specs=[
            pl.BlockSpec((tm, d), lambda i: (i, 0)),
            _resident((1, d)),
            pl.BlockSpec((d, dg), lambda i: (0, ublk), pipeline_mode=pl.Buffered(1)),
            pl.BlockSpec((d, dg), lambda i: (0, ublk + 1), pipeline_mode=pl.Buffered(1)),
            _resident((1, dg)),
            _resident(ws.shape),
            _resident(bs_full.shape),
            _resident((1, dg)),
        ],
        out_specs=pl.BlockSpec((tm, dg), lambda i: (i, 0)),
        out_shape=jax.ShapeDtypeStruct((m, dg), BF16),
        compiler_params=pltpu.CompilerParams(
            dimension_semantics=("parallel",),
            vmem_limit_bytes=48 << 20,
        ),
        name="gmlp",
    )(h, gmix.reshape(1, d), win, win, gv.reshape(1, dg), ws, bs_full, gout.reshape(1, dg))


def _outproj_kernel(h_ref, ys_ref, yg_ref, wo_ref, o_ref, *, ds):
    acc = jnp.dot(ys_ref[...], wo_ref[:ds, :], preferred_element_type=F32)
    acc += jnp.dot(yg_ref[...], wo_ref[ds:, :], preferred_element_type=F32)
    o_ref[...] = h_ref[...] + acc


def _outproj(h, ys, yg, wo, *, tm):
    m, d = h.shape
    ds = ys.shape[1]
    dg = yg.shape[1]
    kern = functools.partial(_outproj_kernel, ds=ds)
    return pl.pallas_call(
        kern,
        grid=(m // tm,),
        in_specs=[
            pl.BlockSpec((tm, d), lambda i: (i, 0)),
            pl.BlockSpec((tm, ds), lambda i: (i, 0)),
            pl.BlockSpec((tm, dg), lambda i: (i, 0)),
            _resident(wo.shape),
        ],
        out_specs=pl.BlockSpec((tm, d), lambda i: (i, 0)),
        out_shape=jax.ShapeDtypeStruct((m, d), F32),
        compiler_params=pltpu.CompilerParams(
            dimension_semantics=("parallel",),
            vmem_limit_bytes=48 << 20,
        ),
        name="outproj",
    )(h, ys, yg, wo)


def _ple_kernel(h_ref, p_ref, gple_ref, wg_ref, wp_ref, gfin_ref, o_ref):
    h = h_ref[...]
    hn = _rms(h, gple_ref[...]).astype(BF16)
    gate = jax.nn.sigmoid(jnp.dot(hn, wg_ref[...], preferred_element_type=F32))
    pp = jnp.dot(p_ref[...].astype(BF16), wp_ref[...], preferred_element_type=F32)
    o_ref[...] = _rms(h + gate * pp, gfin_ref[...])


def _ple(h, p, gple, wg, wp, gfin, *, tm):
    m, d = h.shape
    dp = p.shape[1]
    return pl.pallas_call(
        _ple_kernel,
        grid=(m // tm,),
        in_specs=[
            pl.BlockSpec((tm, d), lambda i: (i, 0)),
            pl.BlockSpec((tm, dp), lambda i: (i, 0)),
            _resident((1, d)),
            _resident(wg.shape),
            _resident(wp.shape),
            _resident((1, d)),
        ],
        out_specs=pl.BlockSpec((tm, d), lambda i: (i, 0)),
        out_shape=jax.ShapeDtypeStruct((m, d), F32),
        compiler_params=pltpu.CompilerParams(
            dimension_semantics=("parallel",),
            vmem_limit_bytes=48 << 20,
        ),
        name="ple",
    )(h, p, gple.reshape(1, d), wg, wp, gfin.reshape(1, d))


def kernel(x, p, norm_ffn1, w1_gate, w1_up, w1_down, norm_mix, w_in, ssm_log_dt, ssm_a_re, ssm_a_im, ssm_b_re, ssm_b_im, ssm_c_re, ssm_c_im, ssm_d, ssm_w_glu, gmlp_norm_v, gmlp_w_s, gmlp_b_s, norm_ssm_out, norm_gmlp_out, w_out, norm_ffn2, w2_gate, w2_up, w2_down, norm_ple, w_ple_gate, w_ple_proj, norm_final):
    bsz, seqlen, d = x.shape
    depth = p.shape[0]
    m = bsz * seqlen
    dssm = ssm_d.shape[1]
    assert depth == 1, "the per-layer embedding kernel fuses the final norm"
    h = x.reshape(m, d)
    for i in range(depth):
        nrow, ff = m // FFN_TM, w1_gate.shape[2]
        w1 = (w1_gate[i], w1_up[i], w1_down[i])
        nf_head = ff // FFN_HEAD_TF
        jobs = [_slab_job(w, FFN_HEAD_TILES * nf_head, lambda r, c: r * nf_head + c) for w in w1]
        h_head, w1_b = _ffn(h, norm_ffn1[i], *w1, jobs, tm=FFN_TM, tf=FFN_HEAD_TF, tiles=(0, FFN_HEAD_TILES))
        nf = ff // FFN_TF
        jobs = [_slab_job(w, (nrow - FFN_HEAD_TILES) * nf, lambda r, c: r * nf + c)
                for w in (w2_gate[i], w2_up[i], w2_down[i], w_in[i], ssm_w_glu[i])]
        h, (w2g, w2u, w2d, w_in_b, wglu_b) = _ffn(
            h, norm_ffn1[i], *w1_b, jobs, tm=FFN_TM, tf=FFN_TF,
            tiles=(FFN_HEAD_TILES, nrow - FFN_HEAD_TILES), base=h_head)

        nsteps = bsz * (seqlen // S5_TC) + 1
        jobs = [_slab_job(w, nsteps) for w in (w_out[i], w_ple_gate[i], w_ple_proj[i])]
        ys, (w_out_b, wpg_b, wpp_b) = _s5(
            h.reshape(bsz, seqlen, d), norm_mix[i], w_in_b,
            ssm_log_dt[i], ssm_a_re[i], ssm_a_im[i], ssm_b_re[i], ssm_b_im[i], ssm_c_re[i], ssm_c_im[i],
            ssm_d[i], wglu_b, norm_ssm_out[i], jobs, tc=S5_TC)

        nh, ck = gmlp_b_s.shape[1:]
        bs_full = jnp.broadcast_to(gmlp_b_s[i][:, :, None], (nh, ck, ck))
        yg = _gmlp(h, norm_mix[i], w_in_b, gmlp_norm_v[i], gmlp_w_s[i], bs_full, norm_gmlp_out[i], tm=ROW_TM)

        h = _outproj(h, ys.reshape(m, dssm), yg, w_out_b, tm=ROW_TM)

        h, _ = _ffn(h, norm_ffn2[i], w2g, w2u, w2d, tm=FFN_TM, tf=FFN_TF)

        h = _ple(h, p[i].reshape(m, -1), norm_ple[i], wpg_b, wpp_b, norm_final, tm=ROW_TM)
    return h.reshape(bsz, seqlen, d)
```

```python
import functools
from typing import Callable, NamedTuple

import jax
import jax.numpy as jnp
from jax import lax
from jax.experimental import pallas as pl
from jax.experimental.pallas import tpu as pltpu

F32 = jnp.float32
BF16 = jnp.bfloat16
EPS = 1e-6

V7X_LANES = 128
V7X_SUBLANES = 8
V7X_VMEM_BYTES = 64 * 1024 * 1024

SSM_BLOCK_GROUPS = 16
GMLP_CHUNK = 128
SCAN_ROW_PITCH = 12

FFN_TM = 1024
FFN_TF = 512
FFN_HEAD_TILES = 4
FFN_HEAD_TF = 256
S5_TC = 256
ROW_TM = 512


def _rms(x, g):
    ms = jnp.mean(x * x, axis=-1, keepdims=True)
    return x * lax.rsqrt(ms + EPS) * g


def _resident(shape):
    n = len(shape)
    return pl.BlockSpec(shape, lambda *_: (0,) * n, pipeline_mode=pl.Buffered(1))


class _CastJob(NamedTuple):
    src: jax.Array
    block: tuple
    index_map: Callable


def _cast_specs(jobs):
    specs = [pl.BlockSpec(j.block, j.index_map) for j in jobs]
    shapes = [jax.ShapeDtypeStruct(j.src.shape, BF16) for j in jobs]
    return specs, shapes


def _run_casts(srcs, dsts):
    for src, dst in zip(srcs, dsts):
        dst[...] = src[...].astype(BF16)


BF16_ROWS = 2 * V7X_SUBLANES


def _slab_job(w, nsteps, step_of=lambda s: s):
    r, c = w.shape
    rows = BF16_ROWS
    while r // rows > nsteps:
        rows *= 2
    n = r // rows
    assert r % rows == 0
    return _CastJob(w, (rows, c), lambda *g: (jnp.minimum(step_of(*g), n - 1), 0))


def _ffn_kernel(*refs, n_cast, has_base):
    n = n_cast
    x_ref, g_ref, wg_ref, wu_ref, wd_ref = refs[:5]
    cast_src = refs[5:5 + n]
    outs = refs[5 + n + int(has_base):]
    o_ref, cast_dst, xn_ref = outs[0], outs[1:1 + n], outs[1 + n]

    @pl.when(pl.program_id(1) == 0)
    def _():
        x = x_ref[...]
        xn_ref[...] = _rms(x, g_ref[...]).astype(BF16)
        o_ref[...] = x

    _run_casts(cast_src, cast_dst)
    xn = xn_ref[...]
    gate = jnp.dot(xn, wg_ref[...].astype(BF16), preferred_element_type=F32)
    up = jnp.dot(xn, wu_ref[...].astype(BF16), preferred_element_type=F32)
    act = (gate * jax.nn.sigmoid(gate) * (0.5 * up)).astype(BF16)
    o_ref[...] += jnp.dot(act, wd_ref[...].astype(BF16), preferred_element_type=F32)


def _ffn(x, g, wg, wu, wd, casts=(), *, tm, tf, tiles=None, base=None):
    m, d = x.shape
    f = wg.shape[1]
    first, ntiles = tiles if tiles is not None else (0, m // tm)
    row = pl.BlockSpec((tm, d), lambda i, j: (i + first, 0))
    cast_specs, cast_shapes = _cast_specs(casts)
    wbytes = wg.dtype.itemsize
    vmem = (2 * tm * d * 4) * 2 + tm * d * 2 + 3 * 2 * d * tf * wbytes + 4 * tm * tf * 4
    vmem += sum(c.block[0] * c.block[1] * 2 * (4 + 2) for c in casts)
    n_in = 5 + len(casts)
    outs = pl.pallas_call(
        functools.partial(_ffn_kernel, n_cast=len(casts), has_base=base is not None),
        grid=(ntiles, f // tf),
        in_specs=[
            row,
            pl.BlockSpec((1, d), lambda i, j: (0, 0)),
            pl.BlockSpec((d, tf), lambda i, j: (0, j)),
            pl.BlockSpec((d, tf), lambda i, j: (0, j)),
            pl.BlockSpec((tf, d), lambda i, j: (j, 0)),
        ] + cast_specs + ([pl.BlockSpec(memory_space=pl.ANY)] if base is not None else []),
        out_specs=[row] + cast_specs,
        out_shape=[jax.ShapeDtypeStruct((m, d), F32)] + cast_shapes,
        input_output_aliases={n_in: 0} if base is not None else {},
        scratch_shapes=[pltpu.VMEM((tm, d), BF16)],
        compiler_params=pltpu.CompilerParams(
            dimension_semantics=("parallel", "arbitrary"),
            vmem_limit_bytes=min(vmem + (8 << 20), V7X_VMEM_BYTES - (2 << 20)),
        ),
        name="ffn",
    )(x, g.reshape(1, d), wg, wu, wd, *[c.src for c in casts], *([base] if base is not None else []))
    return outs[0], list(outs[1:])


def _zoh(logdt, a_re, a_im):
    dt = jnp.exp(logdt)
    lr = jnp.minimum(a_re, -1e-4)
    li = a_im
    mag = jnp.exp(lr * dt)
    ang = li * dt
    abr = mag * jnp.cos(ang)
    abi = mag * jnp.sin(ang)
    den = lr * lr + li * li
    xr = abr - 1.0
    xi = abi
    zr = (xr * lr + xi * li) / den
    zi = (xi * lr - xr * li) / den
    return abr, abi, zr, zi


def _s5_build_params(ldt_ref, are_ref, aim_ref, btre_ref, btim_ref, cre_ref, cim_ref,
                     bblk_ref, cblk_ref, ct_ref, abr_ref, abi_ref):
    ngroup, p, n = btre_ref.shape
    bg = SSM_BLOCK_GROUPS
    abr, abi, zr, zi = _zoh(ldt_ref[...], are_ref[...], aim_ref[...])
    abr_ref[...] = abr.reshape(abr_ref.shape)
    abi_ref[...] = abi.reshape(abi_ref.shape)
    per_row = V7X_LANES // n
    zero = jnp.zeros((p, n), F32)

    def place(piece, gi):
        return jnp.concatenate([piece if s == gi % per_row else zero for s in range(per_row)], axis=1)

    bblk_ref[...] = jnp.zeros_like(bblk_ref)
    for k in range(ngroup // bg):
        ct_ref[...] = jnp.zeros_like(ct_ref)
        for gi in range(bg):
            g = k * bg + gi
            row, lo = g // per_row, (g % per_row) * n
            zr_g = zr[row:row + 1, lo:lo + n]
            zi_g = zi[row:row + 1, lo:lo + n]
            br = btre_ref[g]
            bi = btim_ref[g]
            rows = slice(gi * p, (gi + 1) * p)
            re = slice((gi // per_row) * V7X_LANES, (gi // per_row + 1) * V7X_LANES)
            im = slice(bg * n + re.start, bg * n + re.stop)
            bblk_ref[k, rows, re] = place(zr_g * br - zi_g * bi, gi).astype(BF16)
            bblk_ref[k, rows, im] = place(zr_g * bi + zi_g * br, gi).astype(BF16)
            ct_ref[rows, re] = place(cre_ref[g], gi)
            ct_ref[rows, im] = place(-cim_ref[g], gi)
        cblk_ref[k] = ct_ref[...].T.astype(BF16)


def _s5_kernel(*refs, tc, nblk, bch, bst, chunks_per_seq, n_cast):
    h_ref, gmix_ref, win_ref, d_ref, wglu_ref, gout_ref = refs[:6]
    raw_params = refs[6:13]
    cast_src, o_ref, cast_dst = refs[13:13 + n_cast], refs[13 + n_cast], refs[14 + n_cast:14 + 2 * n_cast]
    (dre0_ref, dim0_ref, u0_ref, dre1_ref, dim1_ref, u1_ref, st_ref,
     bblk_ref, cblk_ref, ct_ref, abr_ref, abi_ref) = refs[14 + 2 * n_cast:]
    ntile = bst // V7X_LANES
    nb = tc // V7X_SUBLANES
    step_id = pl.program_id(0)
    bufs = ((dre0_ref, dim0_ref, u0_ref), (dre1_ref, dim1_ref, u1_ref))
    _run_casts(cast_src, cast_dst)

    @pl.when(step_id == 0)
    def _():
        _s5_build_params(*raw_params, bblk_ref, cblk_ref, ct_ref, abr_ref, abi_ref)
        st_ref[...] = jnp.zeros_like(st_ref)
        for ref in bufs[1]:
            ref[...] = jnp.zeros_like(ref)

    def step(wr, rd):
        dre_w, dim_w, u_w = wr
        dre_r, dim_r, u_r = rd

        keep = jnp.where((step_id - 1) % chunks_per_seq == 0, 0.0, 1.0).astype(F32)
        ar = [abr_ref[k] for k in range(nblk)]
        ai = [abi_ref[k] for k in range(nblk)]
        s = [st_ref[i] * keep for i in range(2 * nblk)]
        for tb in range(nb):
            for r in range(V7X_SUBLANES):
                rows = slice(r * SCAN_ROW_PITCH, r * SCAN_ROW_PITCH + ntile)
                for k in range(nblk):
                    sre, sim = s[2 * k], s[2 * k + 1]
                    nre = ar[k] * sre - ai[k] * sim + dre_r[k, tb, rows, :]
                    nim = ar[k] * sim + ai[k] * sre + dim_r[k, tb, rows, :]
                    dre_r[k, tb, rows, :] = nre
                    dim_r[k, tb, rows, :] = nim
                    s[2 * k], s[2 * k + 1] = nre, nim
        for i in range(2 * nblk):
            st_ref[i] = s[i]

        hn = _rms(h_ref[0], gmix_ref[...]).astype(BF16)
        u = jnp.dot(hn, win_ref[...], preferred_element_type=F32)
        u_w[...] = u
        ub = u.astype(BF16)
        for k in range(nblk):
            drv = jnp.dot(ub[:, k * bch:(k + 1) * bch], bblk_ref[k], preferred_element_type=F32)
            for j in range(ntile):
                lo = j * V7X_LANES
                tile = pl.ds(j, V7X_SUBLANES, stride=SCAN_ROW_PITCH)
                dre_w[k, :, tile, :] = drv[:, lo:lo + V7X_LANES].reshape(nb, V7X_SUBLANES, V7X_LANES)
                dim_w[k, :, tile, :] = drv[:, bst + lo:bst + lo + V7X_LANES].reshape(nb, V7X_SUBLANES, V7X_LANES)

        ys = []
        for k in range(nblk):
            parts = []
            for ref in (dre_r, dim_r):
                for j in range(ntile):
                    tile = pl.ds(j, V7X_SUBLANES, stride=SCAN_ROW_PITCH)
                    parts.append(ref[k, :, tile, :].reshape(tc, V7X_LANES))
            lhs = jnp.concatenate(parts, axis=1).astype(BF16)
            ys.append(jnp.dot(lhs, cblk_ref[k], preferred_element_type=F32))
        y = jnp.concatenate(ys, axis=1) + d_ref[...] * u_r[...]
        y = jax.nn.gelu(y)
        y = y * jax.nn.sigmoid(jnp.dot(y.astype(BF16), wglu_ref[...], preferred_element_type=F32))
        o_ref[0] = _rms(y, gout_ref[...]).astype(BF16)

    @pl.when(step_id % 2 == 0)
    def _():
        step(bufs[0], bufs[1])

    @pl.when(step_id % 2 == 1)
    def _():
        step(bufs[1], bufs[0])


def _s5(h3, gmix, win, log_dt, a_re, a_im, b_re, b_im, c_re, c_im, dskip, wglu, gout, casts=(), *, tc):
    b, l, d = h3.shape
    dssm = dskip.shape[0]
    ngroup, nstate, p = b_re.shape
    nblk, bch, bst = ngroup // SSM_BLOCK_GROUPS, SSM_BLOCK_GROUPS * p, SSM_BLOCK_GROUPS * nstate
    ntile = bst // V7X_LANES
    assert ntile == V7X_SUBLANES and tc % V7X_SUBLANES == 0 and V7X_LANES % nstate == 0
    nb = tc // V7X_SUBLANES
    flat = (ngroup * nstate // V7X_LANES, V7X_LANES)
    raw = [jnp.repeat(log_dt, nstate).reshape(flat), a_re.reshape(flat), a_im.reshape(flat),
           jnp.swapaxes(b_re, 1, 2), jnp.swapaxes(b_im, 1, 2), c_re, c_im]
    nch = l // tc
    last = b * nch - 1
    kern = functools.partial(_s5_kernel, tc=tc, nblk=nblk, bch=bch, bst=bst, chunks_per_seq=nch,
                             n_cast=len(casts))
    cast_specs, cast_shapes = _cast_specs(casts)
    dbuf = pltpu.VMEM((nblk, nb, V7X_SUBLANES * SCAN_ROW_PITCH, V7X_LANES), F32)
    ubuf = pltpu.VMEM((tc, dssm), F32)

    def in_chunk(s):
        c = jnp.minimum(s, last)
        return (c // nch, c % nch, 0)

    def out_chunk(s):
        c = jnp.maximum(s - 1, 0)
        return (c // nch, c % nch, 0)

    outs = pl.pallas_call(
        kern,
        grid=(b * nch + 1,),
        in_specs=[
            pl.BlockSpec((1, tc, d), in_chunk),
            _resident((1, d)),
            pl.BlockSpec((d, dssm), lambda s: (0, 0), pipeline_mode=pl.Buffered(1)),
            _resident((1, dssm)),
            _resident(wglu.shape),
            _resident((1, dssm)),
        ] + [_resident(a.shape) for a in raw] + cast_specs,
        out_specs=[pl.BlockSpec((1, tc, dssm), out_chunk)] + cast_specs,
        out_shape=[jax.ShapeDtypeStruct((b, l, dssm), BF16)] + cast_shapes,
        scratch_shapes=[
            dbuf, dbuf, ubuf, dbuf, dbuf, ubuf,
            pltpu.VMEM((2 * nblk, V7X_SUBLANES, V7X_LANES), F32),
            pltpu.VMEM((nblk, bch, 2 * bst), BF16),
            pltpu.VMEM((nblk, 2 * bst, bch), BF16),
            pltpu.VMEM((bch, 2 * bst), F32),
            pltpu.VMEM((nblk, V7X_SUBLANES, V7X_LANES), F32),
            pltpu.VMEM((nblk, V7X_SUBLANES, V7X_LANES), F32),
        ],
        compiler_params=pltpu.CompilerParams(
            dimension_semantics=("arbitrary",),
            vmem_limit_bytes=56 << 20,
        ),
        name="s5",
    )(h3, gmix.reshape(1, d), win, dskip.reshape(1, dssm), wglu, gout.reshape(1, dssm), *raw,
      *[c.src for c in casts])
    return outs[0], list(outs[1:])


def _gmlp_kernel(*refs, tm, dg, n_cast):
    h_ref, gmix_ref, wu_ref, wv_ref, gv_ref, ws_ref, bs_ref, gout_ref = refs[:8]
    cast_src, o_ref, cast_dst = refs[8:8 + n_cast], refs[8 + n_cast], refs[9 + n_cast:]
    _run_casts(cast_src, cast_dst)
    hn = _rms(h_ref[...], gmix_ref[...]).astype(BF16)
    zu = jnp.dot(hn, wu_ref[...], preferred_element_type=F32)
    zv = jnp.dot(hn, wv_ref[...], preferred_element_type=F32)
    u = jax.nn.gelu(zu)
    v = jax.nn.gelu(zv)
    vc = v - jnp.mean(v, axis=-1, keepdims=True)
    vn = vc * lax.rsqrt(jnp.mean(vc * vc, axis=-1, keepdims=True) + EPS) * gv_ref[...]
    vb = vn.astype(BF16)

    nh = dg // GMLP_CHUNK
    t_idx = lax.broadcasted_iota(jnp.int32, (GMLP_CHUNK, GMLP_CHUNK), 0)
    s_idx = lax.broadcasted_iota(jnp.int32, (GMLP_CHUNK, GMLP_CHUNK), 1)
    causal = t_idx >= s_idx
    nchunk = tm // GMLP_CHUNK
    cols = []
    for hd in range(nh):
        c0 = hd * GMLP_CHUNK
        wm = jnp.where(causal, ws_ref[hd], 0.0).astype(BF16)
        v_h = jnp.concatenate([vb[c * GMLP_CHUNK:(c + 1) * GMLP_CHUNK, c0:c0 + GMLP_CHUNK]
                               for c in range(nchunk)], axis=1)
        s_h = jnp.dot(wm, v_h, preferred_element_type=F32)
        cols.append(jnp.concatenate([s_h[:, c * GMLP_CHUNK:(c + 1) * GMLP_CHUNK] + bs_ref[hd]
                                     for c in range(nchunk)], axis=0))
    yg = u * jnp.concatenate(cols, axis=1)
    o_ref[...] = _rms(yg, gout_ref[...]).astype(BF16)


def _gmlp(h, gmix, win, gv, ws, bs_full, gout, casts=(), *, tm):
    m, d = h.shape
    dg = gv.shape[0]
    ublk = (win.shape[1] - 2 * dg) // dg
    kern = functools.partial(_gmlp_kernel, tm=tm, dg=dg, n_cast=len(casts))
    cast_specs, cast_shapes = _cast_specs(casts)
    outs = pl.pallas_call(
        kern,
        grid=(m // tm,),
        in_specs=[
            pl.BlockSpec((tm, d), lambda i: (i, 0)),
            _resident((1, d)),
            pl.BlockSpec((d, dg), lambda i: (0, ublk), pipeline_mode=pl.Buffered(1)),
            pl.BlockSpec((d, dg), lambda i: (0, ublk + 1), pipeline_mode=pl.Buffered(1)),
            _resident((1, dg)),
            _resident(ws.shape),
            _resident(bs_full.shape),
            _resident((1, dg)),
        ] + cast_specs,
        out_specs=[pl.BlockSpec((tm, dg), lambda i: (i, 0))] + cast_specs,
        out_shape=[jax.ShapeDtypeStruct((m, dg), BF16)] + cast_shapes,
        compiler_params=pltpu.CompilerParams(
            dimension_semantics=("parallel",),
            vmem_limit_bytes=48 << 20,
        ),
        name="gmlp",
    )(h, gmix.reshape(1, d), win, win, gv.reshape(1, dg), ws, bs_full, gout.reshape(1, dg),
      *[c.src for c in casts])
    return outs[0], list(outs[1:])


def _outproj_kernel(*refs, ds, n_cast):
    h_ref, ys_ref, yg_ref, wo_ref = refs[:4]
    cast_src, o_ref, cast_dst = refs[4:4 + n_cast], refs[4 + n_cast], refs[5 + n_cast:]
    _run_casts(cast_src, cast_dst)
    acc = jnp.dot(ys_ref[...], wo_ref[:ds, :], preferred_element_type=F32)
    acc += jnp.dot(yg_ref[...], wo_ref[ds:, :], preferred_element_type=F32)
    o_ref[...] = h_ref[...] + acc


def _outproj(h, ys, yg, wo, casts=(), *, tm):
    m, d = h.shape
    ds = ys.shape[1]
    dg = yg.shape[1]
    kern = functools.partial(_outproj_kernel, ds=ds, n_cast=len(casts))
    cast_specs, cast_shapes = _cast_specs(casts)
    outs = pl.pallas_call(
        kern,
        grid=(m // tm,),
        in_specs=[
            pl.BlockSpec((tm, d), lambda i: (i, 0)),
            pl.BlockSpec((tm, ds), lambda i: (i, 0)),
            pl.BlockSpec((tm, dg), lambda i: (i, 0)),
            _resident(wo.shape),
        ] + cast_specs,
        out_specs=[pl.BlockSpec((tm, d), lambda i: (i, 0))] + cast_specs,
        out_shape=[jax.ShapeDtypeStruct((m, d), F32)] + cast_shapes,
        compiler_params=pltpu.CompilerParams(
            dimension_semantics=("parallel",),
            vmem_limit_bytes=48 << 20,
        ),
        name="outproj",
    )(h, ys, yg, wo, *[c.src for c in casts])
    return outs[0], list(outs[1:])


def _ple_kernel(h_ref, p_ref, gple_ref, wg_ref, wp_ref, gfin_ref, o_ref):
    h = h_ref[...]
    hn = _rms(h, gple_ref[...]).astype(BF16)
    gate = jax.nn.sigmoid(jnp.dot(hn, wg_ref[...], preferred_element_type=F32))
    pp = jnp.dot(p_ref[...].astype(BF16), wp_ref[...], preferred_element_type=F32)
    o_ref[...] = _rms(h + gate * pp, gfin_ref[...])


def _ple(h, p, gple, wg, wp, gfin, *, tm):
    m, d = h.shape
    dp = p.shape[1]
    return pl.pallas_call(
        _ple_kernel,
        grid=(m // tm,),
        in_specs=[
            pl.BlockSpec((tm, d), lambda i: (i, 0)),
            pl.BlockSpec((tm, dp), lambda i: (i, 0)),
            _resident((1, d)),
            _resident(wg.shape),
            _resident(wp.shape),
            _resident((1, d)),
        ],
        out_specs=pl.BlockSpec((tm, d), lambda i: (i, 0)),
        out_shape=jax.ShapeDtypeStruct((m, d), F32),
        compiler_params=pltpu.CompilerParams(
            dimension_semantics=("parallel",),
            vmem_limit_bytes=48 << 20,
        ),
        name="ple",
    )(h, p, gple.reshape(1, d), wg, wp, gfin.reshape(1, d))


def kernel(x, p, norm_ffn1, w1_gate, w1_up, w1_down, norm_mix, w_in, ssm_log_dt, ssm_a_re, ssm_a_im, ssm_b_re, ssm_b_im, ssm_c_re, ssm_c_im, ssm_d, ssm_w_glu, gmlp_norm_v, gmlp_w_s, gmlp_b_s, norm_ssm_out, norm_gmlp_out, w_out, norm_ffn2, w2_gate, w2_up, w2_down, norm_ple, w_ple_gate, w_ple_proj, norm_final):
    bsz, seqlen, d = x.shape
    depth = p.shape[0]
    m = bsz * seqlen
    dssm = ssm_d.shape[1]
    assert depth == 1, "the per-layer embedding kernel fuses the final norm"
    h = x.reshape(m, d)
    for i in range(depth):
        nrow, ff = m // FFN_TM, w1_gate.shape[2]
        w1 = (w1_gate[i], w1_up[i], w1_down[i])
        nf_head = ff // FFN_HEAD_TF
        jobs = [_slab_job(w, FFN_HEAD_TILES * nf_head, lambda r, c: r * nf_head + c) for w in w1]
        h_head, w1_b = _ffn(h, norm_ffn1[i], *w1, jobs, tm=FFN_TM, tf=FFN_HEAD_TF, tiles=(0, FFN_HEAD_TILES))
        nf = ff // FFN_TF
        jobs = [_slab_job(w, (nrow - FFN_HEAD_TILES) * nf, lambda r, c: r * nf + c)
                for w in (w_in[i], ssm_w_glu[i])]
        h, (w_in_b, wglu_b) = _ffn(
            h, norm_ffn1[i], *w1_b, jobs, tm=FFN_TM, tf=FFN_TF,
            tiles=(FFN_HEAD_TILES, nrow - FFN_HEAD_TILES), base=h_head)

        nsteps = bsz * (seqlen // S5_TC) + 1
        jobs = [_slab_job(w, nsteps) for w in (w_out[i], w_ple_gate[i], w_ple_proj[i], w2_down[i])]
        ys, (w_out_b, wpg_b, wpp_b, w2d) = _s5(
            h.reshape(bsz, seqlen, d), norm_mix[i], w_in_b,
            ssm_log_dt[i], ssm_a_re[i], ssm_a_im[i], ssm_b_re[i], ssm_b_im[i], ssm_c_re[i], ssm_c_im[i],
            ssm_d[i], wglu_b, norm_ssm_out[i], jobs, tc=S5_TC)

        nh, ck = gmlp_b_s.shape[1:]
        bs_full = jnp.broadcast_to(gmlp_b_s[i][:, :, None], (nh, ck, ck))
        yg, (w2g,) = _gmlp(h, norm_mix[i], w_in_b, gmlp_norm_v[i], gmlp_w_s[i], bs_full, norm_gmlp_out[i],
                           [_slab_job(w2_gate[i], m // ROW_TM)], tm=ROW_TM)

        h, (w2u,) = _outproj(h, ys.reshape(m, dssm), yg, w_out_b, [_slab_job(w2_up[i], m // ROW_TM)], tm=ROW_TM)

        h, _ = _ffn(h, norm_ffn2[i], w2g, w2u, w2d, tm=FFN_TM, tf=FFN_TF)

        h = _ple(h, p[i].reshape(m, -1), norm_ple[i], wpg_b, wpp_b, norm_final, tm=ROW_TM)
    return h.reshape(bsz, seqlen, d)
```

```python
import functools
from typing import Callable, NamedTuple

import jax
import jax.numpy as jnp
from jax import lax
from jax.experimental import pallas as pl
from jax.experimental.pallas import tpu as pltpu

F32 = jnp.float32
BF16 = jnp.bfloat16
EPS = 1e-6

V7X_LANES = 128
V7X_SUBLANES = 8
V7X_VMEM_BYTES = 64 * 1024 * 1024

SSM_BLOCK_GROUPS = 16
GMLP_CHUNK = 128
SCAN_ROW_PITCH = 12

FFN_TM = 1024
FFN_TF = 512
S5_TC = 256
ROW_TM = 512


def _rms(x, g):
    ms = jnp.mean(x * x, axis=-1, keepdims=True)
    return x * lax.rsqrt(ms + EPS) * g


def _resident(shape):
    n = len(shape)
    return pl.BlockSpec(shape, lambda *_: (0,) * n, pipeline_mode=pl.Buffered(1))


class _CastJob(NamedTuple):
    src: jax.Array
    block: tuple
    index_map: Callable
    tile_cols: int = 0


def _cast_specs(jobs):
    ins = [pl.BlockSpec(j.block, j.index_map) for j in jobs]
    outs, shapes = [], []
    for j in jobs:
        (r, c), (br, _) = j.src.shape, j.block
        if j.tile_cols:
            nt = c // j.tile_cols
            outs.append(pl.BlockSpec((nt, br, j.tile_cols), lambda *g, im=j.index_map: (0, im(*g)[0], 0)))
            shapes.append(jax.ShapeDtypeStruct((nt, r, j.tile_cols), BF16))
        else:
            outs.append(pl.BlockSpec(j.block, j.index_map))
            shapes.append(jax.ShapeDtypeStruct((r, c), BF16))
    return ins, outs, shapes


def _run_casts(srcs, dsts):
    for src, dst in zip(srcs, dsts):
        if len(dst.shape) == 3:
            tc = dst.shape[2]
            for t in range(dst.shape[0]):
                dst[t] = src[:, t * tc:(t + 1) * tc].astype(BF16)
        else:
            dst[...] = src[...].astype(BF16)


BF16_ROWS = 2 * V7X_SUBLANES


def _slab_job(w, nsteps, step_of=lambda s: s, tile_cols=0):
    r, c = w.shape
    rows = BF16_ROWS
    while r // rows > nsteps:
        rows *= 2
    n = r // rows
    assert r % rows == 0
    return _CastJob(w, (rows, c), lambda *g: (jnp.minimum(step_of(*g), n - 1), 0), tile_cols)


def _ffn_kernel(*refs, n_cast):
    n = n_cast
    x_ref, g_ref, wg_ref, wu_ref, wd_ref = refs[:5]
    cast_src, o_ref, cast_dst, xn_ref = refs[5:5 + n], refs[5 + n], refs[6 + n:6 + 2 * n], refs[6 + 2 * n]

    @pl.when(pl.program_id(1) == 0)
    def _():
        x = x_ref[...]
        xn_ref[...] = _rms(x, g_ref[...]).astype(BF16)
        o_ref[...] = x

    _run_casts(cast_src, cast_dst)
    xn = xn_ref[...]
    tiled = len(wg_ref.shape) == 3
    gate = jnp.dot(xn, wg_ref[0] if tiled else wg_ref[...], preferred_element_type=F32)
    up = jnp.dot(xn, wu_ref[0] if tiled else wu_ref[...], preferred_element_type=F32)
    act = (gate * jax.nn.sigmoid(gate) * (0.5 * up)).astype(BF16)
    o_ref[...] += jnp.dot(act, wd_ref[...], preferred_element_type=F32)


def _ffn(x, g, wg, wu, wd, casts=(), *, tm, tf):
    m, d = x.shape
    f = wd.shape[0]
    row = pl.BlockSpec((tm, d), lambda i, j: (i, 0))
    if wg.ndim == 3:
        assert wg.shape == (f // tf, d, tf)
        gate_up = pl.BlockSpec((1, d, tf), lambda i, j: (j, 0, 0))
    else:
        gate_up = pl.BlockSpec((d, tf), lambda i, j: (0, j))
    cast_in, cast_out, cast_shapes = _cast_specs(casts)
    vmem = (2 * tm * d * 4) * 2 + tm * d * 2 + 3 * 2 * d * tf * 2 + 4 * tm * tf * 4
    vmem += sum(c.block[0] * c.block[1] * 2 * (4 + 2) for c in casts)
    outs = pl.pallas_call(
        functools.partial(_ffn_kernel, n_cast=len(casts)),
        grid=(m // tm, f // tf),
        in_specs=[
            row,
            pl.BlockSpec((1, d), lambda i, j: (0, 0)),
            gate_up,
            gate_up,
            pl.BlockSpec((tf, d), lambda i, j: (j, 0)),
        ] + cast_in,
        out_specs=[row] + cast_out,
        out_shape=[jax.ShapeDtypeStruct((m, d), F32)] + cast_shapes,
        scratch_shapes=[pltpu.VMEM((tm, d), BF16)],
        compiler_params=pltpu.CompilerParams(
            dimension_semantics=("parallel", "arbitrary"),
            vmem_limit_bytes=min(vmem + (8 << 20), V7X_VMEM_BYTES - (2 << 20)),
        ),
        name="ffn",
    )(x, g.reshape(1, d), wg, wu, wd, *[c.src for c in casts])
    return outs[0], list(outs[1:])


def _zoh(logdt, a_re, a_im):
    dt = jnp.exp(logdt)
    lr = jnp.minimum(a_re, -1e-4)
    li = a_im
    mag = jnp.exp(lr * dt)
    ang = li * dt
    abr = mag * jnp.cos(ang)
    abi = mag * jnp.sin(ang)
    den = lr * lr + li * li
    xr = abr - 1.0
    xi = abi
    zr = (xr * lr + xi * li) / den
    zi = (xi * lr - xr * li) / den
    return abr, abi, zr, zi


def _s5_build_params(ldt_ref, are_ref, aim_ref, btre_ref, btim_ref, cre_ref, cim_ref,
                     bblk_ref, cblk_ref, ct_ref, abr_ref, abi_ref):
    ngroup, p, n = btre_ref.shape
    bg = SSM_BLOCK_GROUPS
    abr, abi, zr, zi = _zoh(ldt_ref[...], are_ref[...], aim_ref[...])
    abr_ref[...] = abr.reshape(abr_ref.shape)
    abi_ref[...] = abi.reshape(abi_ref.shape)
    per_row = V7X_LANES // n
    zero = jnp.zeros((p, n), F32)

    def place(piece, gi):
        return jnp.concatenate([piece if s == gi % per_row else zero for s in range(per_row)], axis=1)

    bblk_ref[...] = jnp.zeros_like(bblk_ref)
    for k in range(ngroup // bg):
        ct_ref[...] = jnp.zeros_like(ct_ref)
        for gi in range(bg):
            g = k * bg + gi
            row, lo = g // per_row, (g % per_row) * n
            zr_g = zr[row:row + 1, lo:lo + n]
            zi_g = zi[row:row + 1, lo:lo + n]
            br = btre_ref[g]
            bi = btim_ref[g]
            rows = slice(gi * p, (gi + 1) * p)
            re = slice((gi // per_row) * V7X_LANES, (gi // per_row + 1) * V7X_LANES)
            im = slice(bg * n + re.start, bg * n + re.stop)
            bblk_ref[k, rows, re] = place(zr_g * br - zi_g * bi, gi).astype(BF16)
            bblk_ref[k, rows, im] = place(zr_g * bi + zi_g * br, gi).astype(BF16)
            ct_ref[rows, re] = place(cre_ref[g], gi)
            ct_ref[rows, im] = place(-cim_ref[g], gi)
        cblk_ref[k] = ct_ref[...].T.astype(BF16)


def _s5_kernel(*refs, tc, nblk, bch, bst, chunks_per_seq, n_cast):
    h_ref, gmix_ref, win_ref, d_ref, wglu_ref, gout_ref = refs[:6]
    raw_params = refs[6:13]
    cast_src, o_ref, cast_dst = refs[13:13 + n_cast], refs[13 + n_cast], refs[14 + n_cast:14 + 2 * n_cast]
    (dre0_ref, dim0_ref, u0_ref, dre1_ref, dim1_ref, u1_ref, st_ref,
     bblk_ref, cblk_ref, ct_ref, abr_ref, abi_ref) = refs[14 + 2 * n_cast:]
    ntile = bst // V7X_LANES
    nb = tc // V7X_SUBLANES
    step_id = pl.program_id(0)
    bufs = ((dre0_ref, dim0_ref, u0_ref), (dre1_ref, dim1_ref, u1_ref))
    _run_casts(cast_src, cast_dst)

    @pl.when(step_id == 0)
    def _():
        _s5_build_params(*raw_params, bblk_ref, cblk_ref, ct_ref, abr_ref, abi_ref)
        st_ref[...] = jnp.zeros_like(st_ref)
        for ref in bufs[1]:
            ref[...] = jnp.zeros_like(ref)

    def step(wr, rd):
        dre_w, dim_w, u_w = wr
        dre_r, dim_r, u_r = rd

        keep = jnp.where((step_id - 1) % chunks_per_seq == 0, 0.0, 1.0).astype(F32)
        ar = [abr_ref[k] for k in range(nblk)]
        ai = [abi_ref[k] for k in range(nblk)]
        s = [st_ref[i] * keep for i in range(2 * nblk)]
        for tb in range(nb):
            for r in range(V7X_SUBLANES):
                rows = slice(r * SCAN_ROW_PITCH, r * SCAN_ROW_PITCH + ntile)
                for k in range(nblk):
                    sre, sim = s[2 * k], s[2 * k + 1]
                    nre = ar[k] * sre - ai[k] * sim + dre_r[k, tb, rows, :]
                    nim = ar[k] * sim + ai[k] * sre + dim_r[k, tb, rows, :]
                    dre_r[k, tb, rows, :] = nre
                    dim_r[k, tb, rows, :] = nim
                    s[2 * k], s[2 * k + 1] = nre, nim
        for i in range(2 * nblk):
            st_ref[i] = s[i]

        hn = _rms(h_ref[0], gmix_ref[...]).astype(BF16)
        u = jnp.dot(hn, win_ref[...], preferred_element_type=F32)
        u_w[...] = u
        ub = u.astype(BF16)
        for k in range(nblk):
            drv = jnp.dot(ub[:, k * bch:(k + 1) * bch], bblk_ref[k], preferred_element_type=F32)
            for j in range(ntile):
                lo = j * V7X_LANES
                tile = pl.ds(j, V7X_SUBLANES, stride=SCAN_ROW_PITCH)
                dre_w[k, :, tile, :] = drv[:, lo:lo + V7X_LANES].reshape(nb, V7X_SUBLANES, V7X_LANES)
                dim_w[k, :, tile, :] = drv[:, bst + lo:bst + lo + V7X_LANES].reshape(nb, V7X_SUBLANES, V7X_LANES)

        ys = []
        for k in range(nblk):
            parts = []
            for ref in (dre_r, dim_r):
                for j in range(ntile):
                    tile = pl.ds(j, V7X_SUBLANES, stride=SCAN_ROW_PITCH)
                    parts.append(ref[k, :, tile, :].reshape(tc, V7X_LANES))
            lhs = jnp.concatenate(parts, axis=1).astype(BF16)
            ys.append(jnp.dot(lhs, cblk_ref[k], preferred_element_type=F32))
        y = jnp.concatenate(ys, axis=1) + d_ref[...] * u_r[...]
        y = jax.nn.gelu(y)
        y = y * jax.nn.sigmoid(jnp.dot(y.astype(BF16), wglu_ref[...], preferred_element_type=F32))
        o_ref[0] = _rms(y, gout_ref[...]).astype(BF16)

    @pl.when(step_id % 2 == 0)
    def _():
        step(bufs[0], bufs[1])

    @pl.when(step_id % 2 == 1)
    def _():
        step(bufs[1], bufs[0])


def _s5(h3, gmix, win, log_dt, a_re, a_im, b_re, b_im, c_re, c_im, dskip, wglu, gout, casts=(), *, tc):
    b, l, d = h3.shape
    dssm = dskip.shape[0]
    ngroup, nstate, p = b_re.shape
    nblk, bch, bst = ngroup // SSM_BLOCK_GROUPS, SSM_BLOCK_GROUPS * p, SSM_BLOCK_GROUPS * nstate
    ntile = bst // V7X_LANES
    assert ntile == V7X_SUBLANES and tc % V7X_SUBLANES == 0 and V7X_LANES % nstate == 0
    nb = tc // V7X_SUBLANES
    flat = (ngroup * nstate // V7X_LANES, V7X_LANES)
    raw = [jnp.repeat(log_dt, nstate).reshape(flat), a_re.reshape(flat), a_im.reshape(flat),
           jnp.swapaxes(b_re, 1, 2), jnp.swapaxes(b_im, 1, 2), c_re, c_im]
    nch = l // tc
    last = b * nch - 1
    kern = functools.partial(_s5_kernel, tc=tc, nblk=nblk, bch=bch, bst=bst, chunks_per_seq=nch,
                             n_cast=len(casts))
    cast_in, cast_out, cast_shapes = _cast_specs(casts)
    dbuf = pltpu.VMEM((nblk, nb, V7X_SUBLANES * SCAN_ROW_PITCH, V7X_LANES), F32)
    ubuf = pltpu.VMEM((tc, dssm), F32)

    def in_chunk(s):
        c = jnp.minimum(s, last)
        return (c // nch, c % nch, 0)

    def out_chunk(s):
        c = jnp.maximum(s - 1, 0)
        return (c // nch, c % nch, 0)

    outs = pl.pallas_call(
        kern,
        grid=(b * nch + 1,),
        in_specs=[
            pl.BlockSpec((1, tc, d), in_chunk),
            _resident((1, d)),
            pl.BlockSpec((d, dssm), lambda s: (0, 0), pipeline_mode=pl.Buffered(1)),
            _resident((1, dssm)),
            _resident(wglu.shape),
            _resident((1, dssm)),
        ] + [_resident(a.shape) for a in raw] + cast_in,
        out_specs=[pl.BlockSpec((1, tc, dssm), out_chunk)] + cast_out,
        out_shape=[jax.ShapeDtypeStruct((b, l, dssm), BF16)] + cast_shapes,
        scratch_shapes=[
            dbuf, dbuf, ubuf, dbuf, dbuf, ubuf,
            pltpu.VMEM((2 * nblk, V7X_SUBLANES, V7X_LANES), F32),
            pltpu.VMEM((nblk, bch, 2 * bst), BF16),
            pltpu.VMEM((nblk, 2 * bst, bch), BF16),
            pltpu.VMEM((bch, 2 * bst), F32),
            pltpu.VMEM((nblk, V7X_SUBLANES, V7X_LANES), F32),
            pltpu.VMEM((nblk, V7X_SUBLANES, V7X_LANES), F32),
        ],
        compiler_params=pltpu.CompilerParams(
            dimension_semantics=("arbitrary",),
            vmem_limit_bytes=56 << 20,
        ),
        name="s5",
    )(h3, gmix.reshape(1, d), win, dskip.reshape(1, dssm), wglu, gout.reshape(1, dssm), *raw,
      *[c.src for c in casts])
    return outs[0], list(outs[1:])


def _gmlp_kernel(h_ref, gmix_ref, wu_ref, wv_ref, gv_ref, ws_ref, bs_ref, gout_ref, o_ref, *, tm, dg):
    hn = _rms(h_ref[...], gmix_ref[...]).astype(BF16)
    zu = jnp.dot(hn, wu_ref[...], preferred_element_type=F32)
    zv = jnp.dot(hn, wv_ref[...], preferred_element_type=F32)
    u = jax.nn.gelu(zu)
    v = jax.nn.gelu(zv)
    vc = v - jnp.mean(v, axis=-1, keepdims=True)
    vn = vc * lax.rsqrt(jnp.mean(vc * vc, axis=-1, keepdims=True) + EPS) * gv_ref[...]
    vb = vn.astype(BF16)

    nh = dg // GMLP_CHUNK
    t_idx = lax.broadcasted_iota(jnp.int32, (GMLP_CHUNK, GMLP_CHUNK), 0)
    s_idx = lax.broadcasted_iota(jnp.int32, (GMLP_CHUNK, GMLP_CHUNK), 1)
    causal = t_idx >= s_idx
    nchunk = tm // GMLP_CHUNK
    cols = []
    for hd in range(nh):
        c0 = hd * GMLP_CHUNK
        wm = jnp.where(causal, ws_ref[hd], 0.0).astype(BF16)
        v_h = jnp.concatenate([vb[c * GMLP_CHUNK:(c + 1) * GMLP_CHUNK, c0:c0 + GMLP_CHUNK]
                               for c in range(nchunk)], axis=1)
        s_h = jnp.dot(wm, v_h, preferred_element_type=F32)
        cols.append(jnp.concatenate([s_h[:, c * GMLP_CHUNK:(c + 1) * GMLP_CHUNK] + bs_ref[hd]
                                     for c in range(nchunk)], axis=0))
    yg = u * jnp.concatenate(cols, axis=1)
    o_ref[...] = _rms(yg, gout_ref[...]).astype(BF16)


def _gmlp(h, gmix, win, gv, ws, bs_full, gout, *, tm):
    m, d = h.shape
    dg = gv.shape[0]
    ublk = (win.shape[1] - 2 * dg) // dg
    kern = functools.partial(_gmlp_kernel, tm=tm, dg=dg)
    return pl.pallas_call(
        kern,
        grid=(m // tm,),
        in_specs=[
            pl.BlockSpec((tm, d), lambda i: (i, 0)),
            _resident((1, d)),
            pl.BlockSpec((d, dg), lambda i: (0, ublk), pipeline_mode=pl.Buffered(1)),
            pl.BlockSpec((d, dg), lambda i: (0, ublk + 1), pipeline_mode=pl.Buffered(1)),
            _resident((1, dg)),
            _resident(ws.shape),
            _resident(bs_full.shape),
            _resident((1, dg)),
        ],
        out_specs=pl.BlockSpec((tm, dg), lambda i: (i, 0)),
        out_shape=jax.ShapeDtypeStruct((m, dg), BF16),
        compiler_params=pltpu.CompilerParams(
            dimension_semantics=("parallel",),
            vmem_limit_bytes=48 << 20,
        ),
        name="gmlp",
    )(h, gmix.reshape(1, d), win, win, gv.reshape(1, dg), ws, bs_full, gout.reshape(1, dg))


def _outproj_kernel(h_ref, ys_ref, yg_ref, wo_ref, o_ref, *, ds):
    acc = jnp.dot(ys_ref[...], wo_ref[:ds, :], preferred_element_type=F32)
    acc += jnp.dot(yg_ref[...], wo_ref[ds:, :], preferred_element_type=F32)
    o_ref[...] = h_ref[...] + acc


def _outproj(h, ys, yg, wo, *, tm):
    m, d = h.shape
    ds = ys.shape[1]
    dg = yg.shape[1]
    kern = functools.partial(_outproj_kernel, ds=ds)
    return pl.pallas_call(
        kern,
        grid=(m // tm,),
        in_specs=[
            pl.BlockSpec((tm, d), lambda i: (i, 0)),
            pl.BlockSpec((tm, ds), lambda i: (i, 0)),
            pl.BlockSpec((tm, dg), lambda i: (i, 0)),
            _resident(wo.shape),
        ],
        out_specs=pl.BlockSpec((tm, d), lambda i: (i, 0)),
        out_shape=jax.ShapeDtypeStruct((m, d), F32),
        compiler_params=pltpu.CompilerParams(
            dimension_semantics=("parallel",),
            vmem_limit_bytes=48 << 20,
        ),
        name="outproj",
    )(h, ys, yg, wo)


def _ple_kernel(h_ref, p_ref, gple_ref, wg_ref, wp_ref, gfin_ref, o_ref):
    h = h_ref[...]
    hn = _rms(h, gple_ref[...]).astype(BF16)
    gate = jax.nn.sigmoid(jnp.dot(hn, wg_ref[...], preferred_element_type=F32))
    pp = jnp.dot(p_ref[...].astype(BF16), wp_ref[...], preferred_element_type=F32)
    o_ref[...] = _rms(h + gate * pp, gfin_ref[...])


def _ple(h, p, gple, wg, wp, gfin, *, tm):
    m, d = h.shape
    dp = p.shape[1]
    return pl.pallas_call(
        _ple_kernel,
        grid=(m // tm,),
        in_specs=[
            pl.BlockSpec((tm, d), lambda i: (i, 0)),
            pl.BlockSpec((tm, dp), lambda i: (i, 0)),
            _resident((1, d)),
            _resident(wg.shape),
            _resident(wp.shape),
            _resident((1, d)),
        ],
        out_specs=pl.BlockSpec((tm, d), lambda i: (i, 0)),
        out_shape=jax.ShapeDtypeStruct((m, d), F32),
        compiler_params=pltpu.CompilerParams(
            dimension_semantics=("parallel",),
            vmem_limit_bytes=48 << 20,
        ),
        name="ple",
    )(h, p, gple.reshape(1, d), wg, wp, gfin.reshape(1, d))


def kernel(x, p, norm_ffn1, w1_gate, w1_up, w1_down, norm_mix, w_in, ssm_log_dt, ssm_a_re, ssm_a_im, ssm_b_re, ssm_b_im, ssm_c_re, ssm_c_im, ssm_d, ssm_w_glu, gmlp_norm_v, gmlp_w_s, gmlp_b_s, norm_ssm_out, norm_gmlp_out, w_out, norm_ffn2, w2_gate, w2_up, w2_down, norm_ple, w_ple_gate, w_ple_proj, norm_final):
    bsz, seqlen, d = x.shape
    depth = p.shape[0]
    m = bsz * seqlen
    dssm = ssm_d.shape[1]
    assert depth == 1, "the per-layer embedding kernel fuses the final norm"
    h = x.reshape(m, d)
    for i in range(depth):
        nrow, nf = m // FFN_TM, w2_gate.shape[2] // FFN_TF
        jobs = [_slab_job(w, nrow * nf, lambda r, c: r * nf + c, tile_cols=tc)
                for w, tc in ((w2_gate[i], FFN_TF), (w2_up[i], FFN_TF), (w2_down[i], 0), (w_in[i], 0),
                              (ssm_w_glu[i], 0))]
        def tiled(w):
            return jnp.swapaxes(w.astype(BF16).reshape(d, nf, FFN_TF), 0, 1)

        h, (w2g, w2u, w2d, w_in_b, wglu_b) = _ffn(
            h, norm_ffn1[i], tiled(w1_gate[i]), tiled(w1_up[i]), w1_down[i].astype(BF16),
            jobs, tm=FFN_TM, tf=FFN_TF)

        nsteps = bsz * (seqlen // S5_TC) + 1
        jobs = [_slab_job(w, nsteps) for w in (w_out[i], w_ple_gate[i], w_ple_proj[i])]
        ys, (w_out_b, wpg_b, wpp_b) = _s5(
            h.reshape(bsz, seqlen, d), norm_mix[i], w_in_b,
            ssm_log_dt[i], ssm_a_re[i], ssm_a_im[i], ssm_b_re[i], ssm_b_im[i], ssm_c_re[i], ssm_c_im[i],
            ssm_d[i], wglu_b, norm_ssm_out[i], jobs, tc=S5_TC)

        nh, ck = gmlp_b_s.shape[1:]
        bs_full = jnp.broadcast_to(gmlp_b_s[i][:, :, None], (nh, ck, ck))
        yg = _gmlp(h, norm_mix[i], w_in_b, gmlp_norm_v[i], gmlp_w_s[i], bs_full, norm_gmlp_out[i], tm=ROW_TM)

        h = _outproj(h, ys.reshape(m, dssm), yg, w_out_b, tm=ROW_TM)

        h, _ = _ffn(h, norm_ffn2[i], w2g, w2u, w2d, tm=FFN_TM, tf=FFN_TF)

        h = _ple(h, p[i].reshape(m, -1), norm_ple[i], wpg_b, wpp_b, norm_final, tm=ROW_TM)
    return h.reshape(bsz, seqlen, d)
```

```python
import functools
from typing import Callable, NamedTuple

import jax
import jax.numpy as jnp
from jax import lax
from jax.experimental import pallas as pl
from jax.experimental.pallas import tpu as pltpu

F32 = jnp.float32
BF16 = jnp.bfloat16
EPS = 1e-6

V7X_LANES = 128
V7X_SUBLANES = 8
V7X_VMEM_BYTES = 64 * 1024 * 1024

SSM_BLOCK_GROUPS = 16
GMLP_CHUNK = 128
SCAN_ROW_PITCH = 12

FFN_TM = 1024
FFN_TF = 512
S5_TC = 256
ROW_TM = 512


def _rms(x, g):
    ms = jnp.mean(x * x, axis=-1, keepdims=True)
    return x * lax.rsqrt(ms + EPS) * g


def _resident(shape):
    n = len(shape)
    return pl.BlockSpec(shape, lambda *_: (0,) * n, pipeline_mode=pl.Buffered(1))


class _CastJob(NamedTuple):
    src: jax.Array
    block: tuple
    index_map: Callable


def _cast_specs(jobs):
    specs = [pl.BlockSpec(j.block, j.index_map) for j in jobs]
    shapes = [jax.ShapeDtypeStruct(j.src.shape, BF16) for j in jobs]
    return specs, shapes


def _run_casts(srcs, dsts):
    for src, dst in zip(srcs, dsts):
        dst[...] = src[...].astype(BF16)


BF16_ROWS = 2 * V7X_SUBLANES


def _slab_job(w, nsteps, step_of=lambda s: s):
    r, c = w.shape
    rows = BF16_ROWS
    while r // rows > nsteps:
        rows *= 2
    n = r // rows
    assert r % rows == 0
    return _CastJob(w, (rows, c), lambda *g: (jnp.minimum(step_of(*g), n - 1), 0))


def _ffn_kernel(*refs, n_cast):
    n = n_cast
    x_ref, g_ref, wg_ref, wu_ref, wd_ref = refs[:5]
    cast_src, o_ref, cast_dst, xn_ref = refs[5:5 + n], refs[5 + n], refs[6 + n:6 + 2 * n], refs[6 + 2 * n]

    @pl.when(pl.program_id(1) == 0)
    def _():
        x = x_ref[...]
        xn_ref[...] = _rms(x, g_ref[...]).astype(BF16)
        o_ref[...] = x

    _run_casts(cast_src, cast_dst)
    xn = xn_ref[...]
    gate = jnp.dot(xn, wg_ref[...], preferred_element_type=F32)
    up = jnp.dot(xn, wu_ref[...], preferred_element_type=F32)
    act = (gate * jax.nn.sigmoid(gate) * (0.5 * up)).astype(BF16)
    o_ref[...] += jnp.dot(act, wd_ref[...], preferred_element_type=F32)


def _ffn(x, g, wg, wu, wd, casts=(), *, tm, tf):
    m, d = x.shape
    f = wg.shape[1]
    row = pl.BlockSpec((tm, d), lambda i, j: (i, 0))
    cast_specs, cast_shapes = _cast_specs(casts)
    vmem = (2 * tm * d * 4) * 2 + tm * d * 2 + 3 * 2 * d * tf * 2 + 4 * tm * tf * 4
    vmem += sum(c.block[0] * c.block[1] * 2 * (4 + 2) for c in casts)
    outs = pl.pallas_call(
        functools.partial(_ffn_kernel, n_cast=len(casts)),
        grid=(m // tm, f // tf),
        in_specs=[
            row,
            pl.BlockSpec((1, d), lambda i, j: (0, 0)),
            pl.BlockSpec((d, tf), lambda i, j: (0, j)),
            pl.BlockSpec((d, tf), lambda i, j: (0, j)),
            pl.BlockSpec((tf, d), lambda i, j: (j, 0)),
        ] + cast_specs,
        out_specs=[row] + cast_specs,
        out_shape=[jax.ShapeDtypeStruct((m, d), F32)] + cast_shapes,
        scratch_shapes=[pltpu.VMEM((tm, d), BF16)],
        compiler_params=pltpu.CompilerParams(
            dimension_semantics=("parallel", "arbitrary"),
            vmem_limit_bytes=min(vmem + (8 << 20), V7X_VMEM_BYTES - (2 << 20)),
        ),
        name="ffn",
    )(x, g.reshape(1, d), wg, wu, wd, *[c.src for c in casts])
    return outs[0], list(outs[1:])


def _zoh(logdt, a_re, a_im):
    dt = jnp.exp(logdt)
    lr = jnp.minimum(a_re, -1e-4)
    li = a_im
    mag = jnp.exp(lr * dt)
    ang = li * dt
    abr = mag * jnp.cos(ang)
    abi = mag * jnp.sin(ang)
    den = lr * lr + li * li
    xr = abr - 1.0
    xi = abi
    zr = (xr * lr + xi * li) / den
    zi = (xi * lr - xr * li) / den
    return abr, abi, zr, zi


def _s5_build_params(ldt_ref, are_ref, aim_ref, btre_ref, btim_ref, cre_ref, cim_ref,
                     bblk_ref, cblk_ref, ct_ref, abr_ref, abi_ref):
    ngroup, p, n = btre_ref.shape
    bg = SSM_BLOCK_GROUPS
    abr, abi, zr, zi = _zoh(ldt_ref[...], are_ref[...], aim_ref[...])
    abr_ref[...] = abr.reshape(abr_ref.shape)
    abi_ref[...] = abi.reshape(abi_ref.shape)
    per_row = V7X_LANES // n
    zero = jnp.zeros((p, n), F32)

    def place(piece, gi):
        return jnp.concatenate([piece if s == gi % per_row else zero for s in range(per_row)], axis=1)

    bblk_ref[...] = jnp.zeros_like(bblk_ref)
    for k in range(ngroup // bg):
        ct_ref[...] = jnp.zeros_like(ct_ref)
        for gi in range(bg):
            g = k * bg + gi
            row, lo = g // per_row, (g % per_row) * n
            zr_g = zr[row:row + 1, lo:lo + n]
            zi_g = zi[row:row + 1, lo:lo + n]
            br = btre_ref[g]
            bi = btim_ref[g]
            rows = slice(gi * p, (gi + 1) * p)
            re = slice((gi // per_row) * V7X_LANES, (gi // per_row + 1) * V7X_LANES)
            im = slice(bg * n + re.start, bg * n + re.stop)
            bblk_ref[k, rows, re] = place(zr_g * br - zi_g * bi, gi).astype(BF16)
            bblk_ref[k, rows, im] = place(zr_g * bi + zi_g * br, gi).astype(BF16)
            ct_ref[rows, re] = place(cre_ref[g], gi)
            ct_ref[rows, im] = place(-cim_ref[g], gi)
        cblk_ref[k] = ct_ref[...].T.astype(BF16)


def _s5_kernel(*refs, tc, nblk, bch, bst, chunks_per_seq, n_cast):
    zs_ref, d_ref, wglu_ref, gout_ref = refs[:4]
    raw_params = refs[4:11]
    cast_src, o_ref, cast_dst = refs[11:11 + n_cast], refs[11 + n_cast], refs[12 + n_cast:12 + 2 * n_cast]
    (dre0_ref, dim0_ref, u0_ref, dre1_ref, dim1_ref, u1_ref, st_ref,
     bblk_ref, cblk_ref, ct_ref, abr_ref, abi_ref) = refs[12 + 2 * n_cast:]
    ntile = bst // V7X_LANES
    nb = tc // V7X_SUBLANES
    step_id = pl.program_id(0)
    bufs = ((dre0_ref, dim0_ref, u0_ref), (dre1_ref, dim1_ref, u1_ref))
    _run_casts(cast_src, cast_dst)

    @pl.when(step_id == 0)
    def _():
        _s5_build_params(*raw_params, bblk_ref, cblk_ref, ct_ref, abr_ref, abi_ref)
        st_ref[...] = jnp.zeros_like(st_ref)
        for ref in bufs[1]:
            ref[...] = jnp.zeros_like(ref)

    def step(wr, rd):
        dre_w, dim_w, u_w = wr
        dre_r, dim_r, u_r = rd

        keep = jnp.where((step_id - 1) % chunks_per_seq == 0, 0.0, 1.0).astype(F32)
        ar = [abr_ref[k] for k in range(nblk)]
        ai = [abi_ref[k] for k in range(nblk)]
        s = [st_ref[i] * keep for i in range(2 * nblk)]
        for tb in range(nb):
            for r in range(V7X_SUBLANES):
                rows = slice(r * SCAN_ROW_PITCH, r * SCAN_ROW_PITCH + ntile)
                for k in range(nblk):
                    sre, sim = s[2 * k], s[2 * k + 1]
                    nre = ar[k] * sre - ai[k] * sim + dre_r[k, tb, rows, :]
                    nim = ar[k] * sim + ai[k] * sre + dim_r[k, tb, rows, :]
                    dre_r[k, tb, rows, :] = nre
                    dim_r[k, tb, rows, :] = nim
                    s[2 * k], s[2 * k + 1] = nre, nim
        for i in range(2 * nblk):
            st_ref[i] = s[i]

        u = zs_ref[0]
        u_w[...] = u
        ub = u.astype(BF16)
        for k in range(nblk):
            drv = jnp.dot(ub[:, k * bch:(k + 1) * bch], bblk_ref[k], preferred_element_type=F32)
            for j in range(ntile):
                lo = j * V7X_LANES
                tile = pl.ds(j, V7X_SUBLANES, stride=SCAN_ROW_PITCH)
                dre_w[k, :, tile, :] = drv[:, lo:lo + V7X_LANES].reshape(nb, V7X_SUBLANES, V7X_LANES)
                dim_w[k, :, tile, :] = drv[:, bst + lo:bst + lo + V7X_LANES].reshape(nb, V7X_SUBLANES, V7X_LANES)

        ys = []
        for k in range(nblk):
            parts = []
            for ref in (dre_r, dim_r):
                for j in range(ntile):
                    tile = pl.ds(j, V7X_SUBLANES, stride=SCAN_ROW_PITCH)
                    parts.append(ref[k, :, tile, :].reshape(tc, V7X_LANES))
            lhs = jnp.concatenate(parts, axis=1).astype(BF16)
            ys.append(jnp.dot(lhs, cblk_ref[k], preferred_element_type=F32))
        y = jnp.concatenate(ys, axis=1) + d_ref[...] * u_r[...]
        y = jax.nn.gelu(y)
        y = y * jax.nn.sigmoid(jnp.dot(y.astype(BF16), wglu_ref[...], preferred_element_type=F32))
        o_ref[0] = _rms(y, gout_ref[...]).astype(BF16)

    @pl.when(step_id % 2 == 0)
    def _():
        step(bufs[0], bufs[1])

    @pl.when(step_id % 2 == 1)
    def _():
        step(bufs[1], bufs[0])


def _s5(zs3, log_dt, a_re, a_im, b_re, b_im, c_re, c_im, dskip, wglu, gout, casts=(), *, tc):
    b, l, dssm = zs3.shape
    ngroup, nstate, p = b_re.shape
    nblk, bch, bst = ngroup // SSM_BLOCK_GROUPS, SSM_BLOCK_GROUPS * p, SSM_BLOCK_GROUPS * nstate
    ntile = bst // V7X_LANES
    assert ntile == V7X_SUBLANES and tc % V7X_SUBLANES == 0 and V7X_LANES % nstate == 0
    nb = tc // V7X_SUBLANES
    flat = (ngroup * nstate // V7X_LANES, V7X_LANES)
    raw = [jnp.repeat(log_dt, nstate).reshape(flat), a_re.reshape(flat), a_im.reshape(flat),
           jnp.swapaxes(b_re, 1, 2), jnp.swapaxes(b_im, 1, 2), c_re, c_im]
    nch = l // tc
    last = b * nch - 1
    kern = functools.partial(_s5_kernel, tc=tc, nblk=nblk, bch=bch, bst=bst, chunks_per_seq=nch,
                             n_cast=len(casts))
    cast_specs, cast_shapes = _cast_specs(casts)
    dbuf = pltpu.VMEM((nblk, nb, V7X_SUBLANES * SCAN_ROW_PITCH, V7X_LANES), F32)
    ubuf = pltpu.VMEM((tc, dssm), F32)

    def in_chunk(s):
        c = jnp.minimum(s, last)
        return (c // nch, c % nch, 0)

    def out_chunk(s):
        c = jnp.maximum(s - 1, 0)
        return (c // nch, c % nch, 0)

    outs = pl.pallas_call(
        kern,
        grid=(b * nch + 1,),
        in_specs=[
            pl.BlockSpec((1, tc, dssm), in_chunk),
            _resident((1, dssm)),
            _resident(wglu.shape),
            _resident((1, dssm)),
        ] + [_resident(a.shape) for a in raw] + cast_specs,
        out_specs=[pl.BlockSpec((1, tc, dssm), out_chunk)] + cast_specs,
        out_shape=[jax.ShapeDtypeStruct((b, l, dssm), BF16)] + cast_shapes,
        scratch_shapes=[
            dbuf, dbuf, ubuf, dbuf, dbuf, ubuf,
            pltpu.VMEM((2 * nblk, V7X_SUBLANES, V7X_LANES), F32),
            pltpu.VMEM((nblk, bch, 2 * bst), BF16),
            pltpu.VMEM((nblk, 2 * bst, bch), BF16),
            pltpu.VMEM((bch, 2 * bst), F32),
            pltpu.VMEM((nblk, V7X_SUBLANES, V7X_LANES), F32),
            pltpu.VMEM((nblk, V7X_SUBLANES, V7X_LANES), F32),
        ],
        compiler_params=pltpu.CompilerParams(
            dimension_semantics=("arbitrary",),
            vmem_limit_bytes=56 << 20,
        ),
        name="s5",
    )(zs3, dskip.reshape(1, dssm), wglu, gout.reshape(1, dssm), *raw,
      *[c.src for c in casts])
    return outs[0], list(outs[1:])


def _gmlp_kernel(h_ref, gmix_ref, wssm_ref, wu_ref, wv_ref, gv_ref, ws_ref, bs_ref, gout_ref, o_ref, zs_ref,
                 *, tm, dg):
    hn = _rms(h_ref[...], gmix_ref[...]).astype(BF16)
    zs_ref[...] = jnp.dot(hn, wssm_ref[...], preferred_element_type=F32)
    zu = jnp.dot(hn, wu_ref[...], preferred_element_type=F32)
    zv = jnp.dot(hn, wv_ref[...], preferred_element_type=F32)
    u = jax.nn.gelu(zu)
    v = jax.nn.gelu(zv)
    vc = v - jnp.mean(v, axis=-1, keepdims=True)
    vn = vc * lax.rsqrt(jnp.mean(vc * vc, axis=-1, keepdims=True) + EPS) * gv_ref[...]
    vb = vn.astype(BF16)

    nh = dg // GMLP_CHUNK
    t_idx = lax.broadcasted_iota(jnp.int32, (GMLP_CHUNK, GMLP_CHUNK), 0)
    s_idx = lax.broadcasted_iota(jnp.int32, (GMLP_CHUNK, GMLP_CHUNK), 1)
    causal = t_idx >= s_idx
    nchunk = tm // GMLP_CHUNK
    cols = []
    for hd in range(nh):
        c0 = hd * GMLP_CHUNK
        wm = jnp.where(causal, ws_ref[hd], 0.0).astype(BF16)
        v_h = jnp.concatenate([vb[c * GMLP_CHUNK:(c + 1) * GMLP_CHUNK, c0:c0 + GMLP_CHUNK]
                               for c in range(nchunk)], axis=1)
        s_h = jnp.dot(wm, v_h, preferred_element_type=F32)
        cols.append(jnp.concatenate([s_h[:, c * GMLP_CHUNK:(c + 1) * GMLP_CHUNK] + bs_ref[hd]
                                     for c in range(nchunk)], axis=0))
    yg = u * jnp.concatenate(cols, axis=1)
    o_ref[...] = _rms(yg, gout_ref[...]).astype(BF16)


def _gmlp(h, gmix, win, gv, ws, bs_full, gout, *, tm):
    m, d = h.shape
    dg = gv.shape[0]
    dssm = win.shape[1] - 2 * dg
    ublk = dssm // dg
    kern = functools.partial(_gmlp_kernel, tm=tm, dg=dg)
    return pl.pallas_call(
        kern,
        grid=(m // tm,),
        in_specs=[
            pl.BlockSpec((tm, d), lambda i: (i, 0)),
            _resident((1, d)),
            pl.BlockSpec((d, dssm), lambda i: (0, 0), pipeline_mode=pl.Buffered(1)),
            pl.BlockSpec((d, dg), lambda i: (0, ublk), pipeline_mode=pl.Buffered(1)),
            pl.BlockSpec((d, dg), lambda i: (0, ublk + 1), pipeline_mode=pl.Buffered(1)),
            _resident((1, dg)),
            _resident(ws.shape),
            _resident(bs_full.shape),
            _resident((1, dg)),
        ],
        out_specs=[pl.BlockSpec((tm, dg), lambda i: (i, 0)), pl.BlockSpec((tm, dssm), lambda i: (i, 0))],
        out_shape=[jax.ShapeDtypeStruct((m, dg), BF16), jax.ShapeDtypeStruct((m, dssm), F32)],
        compiler_params=pltpu.CompilerParams(
            dimension_semantics=("parallel",),
            vmem_limit_bytes=56 << 20,
        ),
        name="gmlp",
    )(h, gmix.reshape(1, d), win, win, win, gv.reshape(1, dg), ws, bs_full, gout.reshape(1, dg))


def _outproj_kernel(h_ref, ys_ref, yg_ref, wo_ref, o_ref, *, ds):
    acc = jnp.dot(ys_ref[...], wo_ref[:ds, :], preferred_element_type=F32)
    acc += jnp.dot(yg_ref[...], wo_ref[ds:, :], preferred_element_type=F32)
    o_ref[...] = h_ref[...] + acc


def _outproj(h, ys, yg, wo, *, tm):
    m, d = h.shape
    ds = ys.shape[1]
    dg = yg.shape[1]
    kern = functools.partial(_outproj_kernel, ds=ds)
    return pl.pallas_call(
        kern,
        grid=(m // tm,),
        in_specs=[
            pl.BlockSpec((tm, d), lambda i: (i, 0)),
            pl.BlockSpec((tm, ds), lambda i: (i, 0)),
            pl.BlockSpec((tm, dg), lambda i: (i, 0)),
            _resident(wo.shape),
        ],
        out_specs=pl.BlockSpec((tm, d), lambda i: (i, 0)),
        out_shape=jax.ShapeDtypeStruct((m, d), F32),
        compiler_params=pltpu.CompilerParams(
            dimension_semantics=("parallel",),
            vmem_limit_bytes=48 << 20,
        ),
        name="outproj",
    )(h, ys, yg, wo)


def _ple_kernel(h_ref, p_ref, gple_ref, wg_ref, wp_ref, gfin_ref, o_ref):
    h = h_ref[...]
    hn = _rms(h, gple_ref[...]).astype(BF16)
    gate = jax.nn.sigmoid(jnp.dot(hn, wg_ref[...], preferred_element_type=F32))
    pp = jnp.dot(p_ref[...].astype(BF16), wp_ref[...], preferred_element_type=F32)
    o_ref[...] = _rms(h + gate * pp, gfin_ref[...])


def _ple(h, p, gple, wg, wp, gfin, *, tm):
    m, d = h.shape
    dp = p.shape[1]
    return pl.pallas_call(
        _ple_kernel,
        grid=(m // tm,),
        in_specs=[
            pl.BlockSpec((tm, d), lambda i: (i, 0)),
            pl.BlockSpec((tm, dp), lambda i: (i, 0)),
            _resident((1, d)),
            _resident(wg.shape),
            _resident(wp.shape),
            _resident((1, d)),
        ],
        out_specs=pl.BlockSpec((tm, d), lambda i: (i, 0)),
        out_shape=jax.ShapeDtypeStruct((m, d), F32),
        compiler_params=pltpu.CompilerParams(
            dimension_semantics=("parallel",),
            vmem_limit_bytes=48 << 20,
        ),
        name="ple",
    )(h, p, gple.reshape(1, d), wg, wp, gfin.reshape(1, d))


def kernel(x, p, norm_ffn1, w1_gate, w1_up, w1_down, norm_mix, w_in, ssm_log_dt, ssm_a_re, ssm_a_im, ssm_b_re, ssm_b_im, ssm_c_re, ssm_c_im, ssm_d, ssm_w_glu, gmlp_norm_v, gmlp_w_s, gmlp_b_s, norm_ssm_out, norm_gmlp_out, w_out, norm_ffn2, w2_gate, w2_up, w2_down, norm_ple, w_ple_gate, w_ple_proj, norm_final):
    bsz, seqlen, d = x.shape
    depth = p.shape[0]
    m = bsz * seqlen
    dssm = ssm_d.shape[1]
    assert depth == 1, "the per-layer embedding kernel fuses the final norm"
    h = x.reshape(m, d)
    for i in range(depth):
        nrow, nf = m // FFN_TM, w2_gate.shape[2] // FFN_TF
        jobs = [_slab_job(w, nrow * nf, lambda r, c: r * nf + c)
                for w in (w2_gate[i], w2_up[i], w2_down[i], w_in[i], ssm_w_glu[i])]
        h, (w2g, w2u, w2d, w_in_b, wglu_b) = _ffn(
            h, norm_ffn1[i], w1_gate[i].astype(BF16), w1_up[i].astype(BF16), w1_down[i].astype(BF16),
            jobs, tm=FFN_TM, tf=FFN_TF)

        nh, ck = gmlp_b_s.shape[1:]
        bs_full = jnp.broadcast_to(gmlp_b_s[i][:, :, None], (nh, ck, ck))
        yg, zs = _gmlp(h, norm_mix[i], w_in_b, gmlp_norm_v[i], gmlp_w_s[i], bs_full, norm_gmlp_out[i], tm=ROW_TM)

        nsteps = bsz * (seqlen // S5_TC) + 1
        jobs = [_slab_job(w, nsteps) for w in (w_out[i], w_ple_gate[i], w_ple_proj[i])]
        ys, (w_out_b, wpg_b, wpp_b) = _s5(
            zs.reshape(bsz, seqlen, dssm),
            ssm_log_dt[i], ssm_a_re[i], ssm_a_im[i], ssm_b_re[i], ssm_b_im[i], ssm_c_re[i], ssm_c_im[i],
            ssm_d[i], wglu_b, norm_ssm_out[i], jobs, tc=S5_TC)

        h = _outproj(h, ys.reshape(m, dssm), yg, w_out_b, tm=ROW_TM)

        h, _ = _ffn(h, norm_ffn2[i], w2g, w2u, w2d, tm=FFN_TM, tf=FFN_TF)

        h = _ple(h, p[i].reshape(m, -1), norm_ple[i], wpg_b, wpp_b, norm_final, tm=ROW_TM)
    return h.reshape(bsz, seqlen, d)
```

```python
import functools
from typing import Callable, NamedTuple

import jax
import jax.numpy as jnp
from jax import lax
from jax.experimental import pallas as pl
from jax.experimental.pallas import tpu as pltpu

F32 = jnp.float32
BF16 = jnp.bfloat16
EPS = 1e-6

V7X_LANES = 128
V7X_SUBLANES = 8
V7X_VMEM_BYTES = 64 * 1024 * 1024

SSM_BLOCK_GROUPS = 16
GMLP_CHUNK = 128
SCAN_ROW_PITCH = 12

FFN_TM = 1024
FFN_TF = 512
S5_TC = 256
ROW_TM = 512


def _rms(x, g):
    ms = jnp.mean(x * x, axis=-1, keepdims=True)
    return x * lax.rsqrt(ms + EPS) * g


def _resident(shape):
    n = len(shape)
    return pl.BlockSpec(shape, lambda *_: (0,) * n, pipeline_mode=pl.Buffered(1))


class _CastJob(NamedTuple):
    src: jax.Array
    block: tuple
    index_map: Callable


def _cast_specs(jobs):
    specs = [pl.BlockSpec(j.block, j.index_map) for j in jobs]
    shapes = [jax.ShapeDtypeStruct(j.src.shape, BF16) for j in jobs]
    return specs, shapes


def _run_casts(srcs, dsts):
    for src, dst in zip(srcs, dsts):
        dst[...] = src[...].astype(BF16)


BF16_ROWS = 2 * V7X_SUBLANES


def _slab_job(w, nsteps, step_of=lambda s: s):
    r, c = w.shape
    rows = BF16_ROWS
    while r // rows > nsteps:
        rows *= 2
    n = r // rows
    assert r % rows == 0
    return _CastJob(w, (rows, c), lambda *g: (jnp.minimum(step_of(*g), n - 1), 0))


def _ffn_kernel(*refs, n_cast):
    n = n_cast
    x_ref, g_ref, wg_ref, wu_ref, wd_ref = refs[:5]
    cast_src, o_ref, cast_dst, xn_ref = refs[5:5 + n], refs[5 + n], refs[6 + n:6 + 2 * n], refs[6 + 2 * n]

    @pl.when(pl.program_id(1) == 0)
    def _():
        x = x_ref[...]
        xn_ref[...] = _rms(x, g_ref[...]).astype(BF16)
        o_ref[...] = x

    _run_casts(cast_src, cast_dst)
    xn = xn_ref[...]
    gate = jnp.dot(xn, wg_ref[...], preferred_element_type=F32)
    up = jnp.dot(xn, wu_ref[...], preferred_element_type=F32)
    act = (gate * jax.nn.sigmoid(gate) * (0.5 * up)).astype(BF16)
    o_ref[...] += jnp.dot(act, wd_ref[...], preferred_element_type=F32)


def _ffn(x, g, wg, wu, wd, casts=(), *, tm, tf):
    m, d = x.shape
    f = wg.shape[1]
    row = pl.BlockSpec((tm, d), lambda i, j: (i, 0))
    cast_specs, cast_shapes = _cast_specs(casts)
    vmem = (2 * tm * d * 4) * 2 + tm * d * 2 + 3 * 2 * d * tf * 2 + 4 * tm * tf * 4
    vmem += sum(c.block[0] * c.block[1] * 2 * (4 + 2) for c in casts)
    outs = pl.pallas_call(
        functools.partial(_ffn_kernel, n_cast=len(casts)),
        grid=(m // tm, f // tf),
        in_specs=[
            row,
            pl.BlockSpec((1, d), lambda i, j: (0, 0)),
            pl.BlockSpec((d, tf), lambda i, j: (0, j)),
            pl.BlockSpec((d, tf), lambda i, j: (0, j)),
            pl.BlockSpec((tf, d), lambda i, j: (j, 0)),
        ] + cast_specs,
        out_specs=[row] + cast_specs,
        out_shape=[jax.ShapeDtypeStruct((m, d), F32)] + cast_shapes,
        scratch_shapes=[pltpu.VMEM((tm, d), BF16)],
        compiler_params=pltpu.CompilerParams(
            dimension_semantics=("parallel", "arbitrary"),
            vmem_limit_bytes=min(vmem + (8 << 20), V7X_VMEM_BYTES - (2 << 20)),
        ),
        name="ffn",
    )(x, g.reshape(1, d), wg, wu, wd, *[c.src for c in casts])
    return outs[0], list(outs[1:])


def _zoh(logdt, a_re, a_im):
    dt = jnp.exp(logdt)
    lr = jnp.minimum(a_re, -1e-4)
    li = a_im
    mag = jnp.exp(lr * dt)
    ang = li * dt
    abr = mag * jnp.cos(ang)
    abi = mag * jnp.sin(ang)
    den = lr * lr + li * li
    xr = abr - 1.0
    xi = abi
    zr = (xr * lr + xi * li) / den
    zi = (xi * lr - xr * li) / den
    return abr, abi, zr, zi


def _s5_build_params(ldt_ref, are_ref, aim_ref, btre_ref, btim_ref, cre_ref, cim_ref,
                     bblk_ref, cblk_ref, ct_ref, abr_ref, abi_ref):
    ngroup, p, n = btre_ref.shape
    bg = SSM_BLOCK_GROUPS
    abr, abi, zr, zi = _zoh(ldt_ref[...], are_ref[...], aim_ref[...])
    abr_ref[...] = abr.reshape(abr_ref.shape)
    abi_ref[...] = abi.reshape(abi_ref.shape)
    per_row = V7X_LANES // n
    zero = jnp.zeros((p, n), F32)

    def place(piece, gi):
        return jnp.concatenate([piece if s == gi % per_row else zero for s in range(per_row)], axis=1)

    bblk_ref[...] = jnp.zeros_like(bblk_ref)
    for k in range(ngroup // bg):
        ct_ref[...] = jnp.zeros_like(ct_ref)
        for gi in range(bg):
            g = k * bg + gi
            row, lo = g // per_row, (g % per_row) * n
            zr_g = zr[row:row + 1, lo:lo + n]
            zi_g = zi[row:row + 1, lo:lo + n]
            br = btre_ref[g]
            bi = btim_ref[g]
            rows = slice(gi * p, (gi + 1) * p)
            re = slice((gi // per_row) * V7X_LANES, (gi // per_row + 1) * V7X_LANES)
            im = slice(bg * n + re.start, bg * n + re.stop)
            bblk_ref[k, rows, re] = place(zr_g * br - zi_g * bi, gi).astype(BF16)
            bblk_ref[k, rows, im] = place(zr_g * bi + zi_g * br, gi).astype(BF16)
            ct_ref[rows, re] = place(cre_ref[g], gi)
            ct_ref[rows, im] = place(-cim_ref[g], gi)
        cblk_ref[k] = ct_ref[...].T.astype(BF16)


def _s5_kernel(*refs, tc, nblk, bch, bst, chunks_per_seq, n_cast):
    zs_ref, d_ref, wglu_ref, gout_ref = refs[:4]
    raw_params = refs[4:11]
    cast_src, o_ref, cast_dst = refs[11:11 + n_cast], refs[11 + n_cast], refs[12 + n_cast:12 + 2 * n_cast]
    (dre0_ref, dim0_ref, u0_ref, dre1_ref, dim1_ref, u1_ref, st_ref,
     bblk_ref, cblk_ref, ct_ref, abr_ref, abi_ref) = refs[12 + 2 * n_cast:]
    ntile = bst // V7X_LANES
    nb = tc // V7X_SUBLANES
    step_id = pl.program_id(0)
    bufs = ((dre0_ref, dim0_ref, u0_ref), (dre1_ref, dim1_ref, u1_ref))
    _run_casts(cast_src, cast_dst)

    @pl.when(step_id == 0)
    def _():
        _s5_build_params(*raw_params, bblk_ref, cblk_ref, ct_ref, abr_ref, abi_ref)
        st_ref[...] = jnp.zeros_like(st_ref)
        for ref in bufs[1]:
            ref[...] = jnp.zeros_like(ref)

    def step(wr, rd):
        dre_w, dim_w, u_w = wr
        dre_r, dim_r, u_r = rd

        keep = jnp.where((step_id - 1) % chunks_per_seq == 0, 0.0, 1.0).astype(F32)
        ar = [abr_ref[k] for k in range(nblk)]
        ai = [abi_ref[k] for k in range(nblk)]
        s = [st_ref[i] * keep for i in range(2 * nblk)]
        for tb in range(nb):
            for r in range(V7X_SUBLANES):
                rows = slice(r * SCAN_ROW_PITCH, r * SCAN_ROW_PITCH + ntile)
                for k in range(nblk):
                    sre, sim = s[2 * k], s[2 * k + 1]
                    nre = ar[k] * sre - ai[k] * sim + dre_r[k, tb, rows, :]
                    nim = ar[k] * sim + ai[k] * sre + dim_r[k, tb, rows, :]
                    dre_r[k, tb, rows, :] = nre
                    dim_r[k, tb, rows, :] = nim
                    s[2 * k], s[2 * k + 1] = nre, nim
        for i in range(2 * nblk):
            st_ref[i] = s[i]

        u = zs_ref[0]
        u_w[...] = u
        ub = u.astype(BF16)
        for k in range(nblk):
            drv = jnp.dot(ub[:, k * bch:(k + 1) * bch], bblk_ref[k], preferred_element_type=F32)
            for j in range(ntile):
                lo = j * V7X_LANES
                tile = pl.ds(j, V7X_SUBLANES, stride=SCAN_ROW_PITCH)
                dre_w[k, :, tile, :] = drv[:, lo:lo + V7X_LANES].reshape(nb, V7X_SUBLANES, V7X_LANES)
                dim_w[k, :, tile, :] = drv[:, bst + lo:bst + lo + V7X_LANES].reshape(nb, V7X_SUBLANES, V7X_LANES)

        ys = []
        for k in range(nblk):
            parts = []
            for ref in (dre_r, dim_r):
                for j in range(ntile):
                    tile = pl.ds(j, V7X_SUBLANES, stride=SCAN_ROW_PITCH)
                    parts.append(ref[k, :, tile, :].reshape(tc, V7X_LANES))
            lhs = jnp.concatenate(parts, axis=1).astype(BF16)
            ys.append(jnp.dot(lhs, cblk_ref[k], preferred_element_type=F32))
        y = jnp.concatenate(ys, axis=1) + d_ref[...] * u_r[...]
        y = jax.nn.gelu(y)
        y = y * jax.nn.sigmoid(jnp.dot(y.astype(BF16), wglu_ref[...], preferred_element_type=F32))
        o_ref[0] = _rms(y, gout_ref[...]).astype(BF16)

    @pl.when(step_id % 2 == 0)
    def _():
        step(bufs[0], bufs[1])

    @pl.when(step_id % 2 == 1)
    def _():
        step(bufs[1], bufs[0])


def _s5(zs3, log_dt, a_re, a_im, b_re, b_im, c_re, c_im, dskip, wglu, gout, casts=(), *, tc):
    b, l, dssm = zs3.shape
    ngroup, nstate, p = b_re.shape
    nblk, bch, bst = ngroup // SSM_BLOCK_GROUPS, SSM_BLOCK_GROUPS * p, SSM_BLOCK_GROUPS * nstate
    ntile = bst // V7X_LANES
    assert ntile == V7X_SUBLANES and tc % V7X_SUBLANES == 0 and V7X_LANES % nstate == 0
    nb = tc // V7X_SUBLANES
    flat = (ngroup * nstate // V7X_LANES, V7X_LANES)
    raw = [jnp.repeat(log_dt, nstate).reshape(flat), a_re.reshape(flat), a_im.reshape(flat),
           jnp.swapaxes(b_re, 1, 2), jnp.swapaxes(b_im, 1, 2), c_re, c_im]
    nch = l // tc
    last = b * nch - 1
    kern = functools.partial(_s5_kernel, tc=tc, nblk=nblk, bch=bch, bst=bst, chunks_per_seq=nch,
                             n_cast=len(casts))
    cast_specs, cast_shapes = _cast_specs(casts)
    dbuf = pltpu.VMEM((nblk, nb, V7X_SUBLANES * SCAN_ROW_PITCH, V7X_LANES), F32)
    ubuf = pltpu.VMEM((tc, dssm), F32)

    def in_chunk(s):
        c = jnp.minimum(s, last)
        return (c // nch, c % nch, 0)

    def out_chunk(s):
        c = jnp.maximum(s - 1, 0)
        return (c // nch, c % nch, 0)

    outs = pl.pallas_call(
        kern,
        grid=(b * nch + 1,),
        in_specs=[
            pl.BlockSpec((1, tc, dssm), in_chunk),
            _resident((1, dssm)),
            _resident(wglu.shape),
            _resident((1, dssm)),
        ] + [_resident(a.shape) for a in raw] + cast_specs,
        out_specs=[pl.BlockSpec((1, tc, dssm), out_chunk)] + cast_specs,
        out_shape=[jax.ShapeDtypeStruct((b, l, dssm), BF16)] + cast_shapes,
        scratch_shapes=[
            dbuf, dbuf, ubuf, dbuf, dbuf, ubuf,
            pltpu.VMEM((2 * nblk, V7X_SUBLANES, V7X_LANES), F32),
            pltpu.VMEM((nblk, bch, 2 * bst), BF16),
            pltpu.VMEM((nblk, 2 * bst, bch), BF16),
            pltpu.VMEM((bch, 2 * bst), F32),
            pltpu.VMEM((nblk, V7X_SUBLANES, V7X_LANES), F32),
            pltpu.VMEM((nblk, V7X_SUBLANES, V7X_LANES), F32),
        ],
        compiler_params=pltpu.CompilerParams(
            dimension_semantics=("arbitrary",),
            vmem_limit_bytes=56 << 20,
        ),
        name="s5",
    )(zs3, dskip.reshape(1, dssm), wglu, gout.reshape(1, dssm), *raw,
      *[c.src for c in casts])
    return outs[0], list(outs[1:])


def _gmlp_kernel(h_ref, gmix_ref, wssm_ref, wu_ref, wv_ref, gv_ref, ws_ref, bs_ref, gout_ref, o_ref, zs_ref,
                 *, tm, dg):
    hn = _rms(h_ref[...], gmix_ref[...]).astype(BF16)
    zv = jnp.dot(hn, wv_ref[...], preferred_element_type=F32)
    zu = jnp.dot(hn, wu_ref[...], preferred_element_type=F32)
    zs_ref[...] = jnp.dot(hn, wssm_ref[...], preferred_element_type=F32)
    u = jax.nn.gelu(zu)
    v = jax.nn.gelu(zv)
    vc = v - jnp.mean(v, axis=-1, keepdims=True)
    vn = vc * lax.rsqrt(jnp.mean(vc * vc, axis=-1, keepdims=True) + EPS) * gv_ref[...]
    vb = vn.astype(BF16)

    nh = dg // GMLP_CHUNK
    t_idx = lax.broadcasted_iota(jnp.int32, (GMLP_CHUNK, GMLP_CHUNK), 0)
    s_idx = lax.broadcasted_iota(jnp.int32, (GMLP_CHUNK, GMLP_CHUNK), 1)
    causal = t_idx >= s_idx
    nchunk = tm // GMLP_CHUNK
    cols = []
    for hd in range(nh):
        c0 = hd * GMLP_CHUNK
        wm = jnp.where(causal, ws_ref[hd], 0.0).astype(BF16)
        v_h = jnp.concatenate([vb[c * GMLP_CHUNK:(c + 1) * GMLP_CHUNK, c0:c0 + GMLP_CHUNK]
                               for c in range(nchunk)], axis=1)
        s_h = jnp.dot(wm, v_h, preferred_element_type=F32)
        cols.append(jnp.concatenate([s_h[:, c * GMLP_CHUNK:(c + 1) * GMLP_CHUNK] + bs_ref[hd]
                                     for c in range(nchunk)], axis=0))
    yg = u * jnp.concatenate(cols, axis=1)
    o_ref[...] = _rms(yg, gout_ref[...]).astype(BF16)


def _gmlp(h, gmix, win, gv, ws, bs_full, gout, *, tm):
    m, d = h.shape
    dg = gv.shape[0]
    dssm = win.shape[1] - 2 * dg
    ublk = dssm // dg
    kern = functools.partial(_gmlp_kernel, tm=tm, dg=dg)
    return pl.pallas_call(
        kern,
        grid=(m // tm,),
        in_specs=[
            pl.BlockSpec((tm, d), lambda i: (i, 0)),
            _resident((1, d)),
            pl.BlockSpec((d, dssm), lambda i: (0, 0), pipeline_mode=pl.Buffered(1)),
            pl.BlockSpec((d, dg), lambda i: (0, ublk), pipeline_mode=pl.Buffered(1)),
            pl.BlockSpec((d, dg), lambda i: (0, ublk + 1), pipeline_mode=pl.Buffered(1)),
            _resident((1, dg)),
            _resident(ws.shape),
            _resident(bs_full.shape),
            _resident((1, dg)),
        ],
        out_specs=[pl.BlockSpec((tm, dg), lambda i: (i, 0)), pl.BlockSpec((tm, dssm), lambda i: (i, 0))],
        out_shape=[jax.ShapeDtypeStruct((m, dg), BF16), jax.ShapeDtypeStruct((m, dssm), F32)],
        compiler_params=pltpu.CompilerParams(
            dimension_semantics=("parallel",),
            vmem_limit_bytes=56 << 20,
        ),
        name="gmlp",
    )(h, gmix.reshape(1, d), win, win, win, gv.reshape(1, dg), ws, bs_full, gout.reshape(1, dg))


def _outproj_kernel(h_ref, ys_ref, yg_ref, wo_ref, o_ref, *, ds):
    acc = jnp.dot(ys_ref[...], wo_ref[:ds, :], preferred_element_type=F32)
    acc += jnp.dot(yg_ref[...], wo_ref[ds:, :], preferred_element_type=F32)
    o_ref[...] = h_ref[...] + acc


def _outproj(h, ys, yg, wo, *, tm):
    m, d = h.shape
    ds = ys.shape[1]
    dg = yg.shape[1]
    kern = functools.partial(_outproj_kernel, ds=ds)
    return pl.pallas_call(
        kern,
        grid=(m // tm,),
        in_specs=[
            pl.BlockSpec((tm, d), lambda i: (i, 0)),
            pl.BlockSpec((tm, ds), lambda i: (i, 0)),
            pl.BlockSpec((tm, dg), lambda i: (i, 0)),
            _resident(wo.shape),
        ],
        out_specs=pl.BlockSpec((tm, d), lambda i: (i, 0)),
        out_shape=jax.ShapeDtypeStruct((m, d), F32),
        compiler_params=pltpu.CompilerParams(
            dimension_semantics=("parallel",),
            vmem_limit_bytes=48 << 20,
        ),
        name="outproj",
    )(h, ys, yg, wo)


def _ple_kernel(h_ref, p_ref, gple_ref, wg_ref, wp_ref, gfin_ref, o_ref):
    h = h_ref[...]
    hn = _rms(h, gple_ref[...]).astype(BF16)
    gate = jax.nn.sigmoid(jnp.dot(hn, wg_ref[...], preferred_element_type=F32))
    pp = jnp.dot(p_ref[...].astype(BF16), wp_ref[...], preferred_element_type=F32)
    o_ref[...] = _rms(h + gate * pp, gfin_ref[...])


def _ple(h, p, gple, wg, wp, gfin, *, tm):
    m, d = h.shape
    dp = p.shape[1]
    return pl.pallas_call(
        _ple_kernel,
        grid=(m // tm,),
        in_specs=[
            pl.BlockSpec((tm, d), lambda i: (i, 0)),
            pl.BlockSpec((tm, dp), lambda i: (i, 0)),
            _resident((1, d)),
            _resident(wg.shape),
            _resident(wp.shape),
            _resident((1, d)),
        ],
        out_specs=pl.BlockSpec((tm, d), lambda i: (i, 0)),
        out_shape=jax.ShapeDtypeStruct((m, d), F32),
        compiler_params=pltpu.CompilerParams(
            dimension_semantics=("parallel",),
            vmem_limit_bytes=48 << 20,
        ),
        name="ple",
    )(h, p, gple.reshape(1, d), wg, wp, gfin.reshape(1, d))


def kernel(x, p, norm_ffn1, w1_gate, w1_up, w1_down, norm_mix, w_in, ssm_log_dt, ssm_a_re, ssm_a_im, ssm_b_re, ssm_b_im, ssm_c_re, ssm_c_im, ssm_d, ssm_w_glu, gmlp_norm_v, gmlp_w_s, gmlp_b_s, norm_ssm_out, norm_gmlp_out, w_out, norm_ffn2, w2_gate, w2_up, w2_down, norm_ple, w_ple_gate, w_ple_proj, norm_final):
    bsz, seqlen, d = x.shape
    depth = p.shape[0]
    m = bsz * seqlen
    dssm = ssm_d.shape[1]
    assert depth == 1, "the per-layer embedding kernel fuses the final norm"
    h = x.reshape(m, d)
    for i in range(depth):
        nrow, nf = m // FFN_TM, w2_gate.shape[2] // FFN_TF
        jobs = [_slab_job(w, nrow * nf, lambda r, c: r * nf + c)
                for w in (w2_gate[i], w2_up[i], w2_down[i], w_in[i], ssm_w_glu[i])]
        h, (w2g, w2u, w2d, w_in_b, wglu_b) = _ffn(
            h, norm_ffn1[i], w1_gate[i].astype(BF16), w1_up[i].astype(BF16), w1_down[i].astype(BF16),
            jobs, tm=FFN_TM, tf=FFN_TF)

        nh, ck = gmlp_b_s.shape[1:]
        bs_full = jnp.broadcast_to(gmlp_b_s[i][:, :, None], (nh, ck, ck))
        yg, zs = _gmlp(h, norm_mix[i], w_in_b, gmlp_norm_v[i], gmlp_w_s[i], bs_full, norm_gmlp_out[i], tm=ROW_TM)

        nsteps = bsz * (seqlen // S5_TC) + 1
        jobs = [_slab_job(w, nsteps) for w in (w_out[i], w_ple_gate[i], w_ple_proj[i])]
        ys, (w_out_b, wpg_b, wpp_b) = _s5(
            zs.reshape(bsz, seqlen, dssm),
            ssm_log_dt[i], ssm_a_re[i], ssm_a_im[i], ssm_b_re[i], ssm_b_im[i], ssm_c_re[i], ssm_c_im[i],
            ssm_d[i], wglu_b, norm_ssm_out[i], jobs, tc=S5_TC)

        h = _outproj(h, ys.reshape(m, dssm), yg, w_out_b, tm=ROW_TM)

        h, _ = _ffn(h, norm_ffn2[i], w2g, w2u, w2d, tm=FFN_TM, tf=FFN_TF)

        h = _ple(h, p[i].reshape(m, -1), norm_ple[i], wpg_b, wpp_b, norm_final, tm=ROW_TM)
    return h.reshape(bsz, seqlen, d)
```

```python
import functools
from typing import Callable, NamedTuple

import jax
import jax.numpy as jnp
from jax import lax
from jax.experimental import pallas as pl
from jax.experimental.pallas import tpu as pltpu

F32 = jnp.float32
BF16 = jnp.bfloat16
EPS = 1e-6

V7X_LANES = 128
V7X_SUBLANES = 8
V7X_VMEM_BYTES = 64 * 1024 * 1024

SSM_BLOCK_GROUPS = 16
GMLP_CHUNK = 128
SCAN_ROW_PITCH = 12

FFN_TM = 1024
FFN_TF = 512
S5_TC = 256
ROW_TM = 512


def _rms(x, g):
    ms = jnp.mean(x * x, axis=-1, keepdims=True)
    return x * lax.rsqrt(ms + EPS) * g


def _resident(shape):
    n = len(shape)
    return pl.BlockSpec(shape, lambda *_: (0,) * n, pipeline_mode=pl.Buffered(1))


class _CastJob(NamedTuple):
    src: jax.Array
    block: tuple
    index_map: Callable


def _cast_specs(jobs):
    specs = [pl.BlockSpec(j.block, j.index_map) for j in jobs]
    shapes = [jax.ShapeDtypeStruct(j.src.shape, BF16) for j in jobs]
    return specs, shapes


def _run_casts(srcs, dsts):
    for src, dst in zip(srcs, dsts):
        dst[...] = src[...].astype(BF16)


BF16_ROWS = 2 * V7X_SUBLANES


def _slab_job(w, nsteps, step_of=lambda s: s):
    r, c = w.shape
    rows = BF16_ROWS
    while r // rows > nsteps:
        rows *= 2
    n = r // rows
    assert r % rows == 0
    return _CastJob(w, (rows, c), lambda *g: (jnp.minimum(step_of(*g), n - 1), 0))


def _ffn_kernel(*refs, n_cast):
    n = n_cast
    x_ref, g_ref, wg_ref, wu_ref, wd_ref = refs[:5]
    cast_src, o_ref, cast_dst, xn_ref = refs[5:5 + n], refs[5 + n], refs[6 + n:6 + 2 * n], refs[6 + 2 * n]

    @pl.when(pl.program_id(1) == 0)
    def _():
        x = x_ref[...]
        xn_ref[...] = _rms(x, g_ref[...]).astype(BF16)
        o_ref[...] = x

    _run_casts(cast_src, cast_dst)
    xn = xn_ref[...]
    gate = jnp.dot(xn, wg_ref[...], preferred_element_type=F32)
    up = jnp.dot(xn, wu_ref[...], preferred_element_type=F32)
    act = (gate * jax.nn.sigmoid(gate) * (0.5 * up)).astype(BF16)
    o_ref[...] += jnp.dot(act, wd_ref[...], preferred_element_type=F32)


def _ffn(x, g, wg, wu, wd, casts=(), *, tm, tf):
    m, d = x.shape
    f = wg.shape[1]
    row = pl.BlockSpec((tm, d), lambda i, j: (i, 0))
    cast_specs, cast_shapes = _cast_specs(casts)
    vmem = (2 * tm * d * 4) * 2 + tm * d * 2 + 3 * 2 * d * tf * 2 + 4 * tm * tf * 4
    vmem += sum(c.block[0] * c.block[1] * 2 * (4 + 2) for c in casts)
    outs = pl.pallas_call(
        functools.partial(_ffn_kernel, n_cast=len(casts)),
        grid=(m // tm, f // tf),
        in_specs=[
            row,
            pl.BlockSpec((1, d), lambda i, j: (0, 0)),
            pl.BlockSpec((d, tf), lambda i, j: (0, j)),
            pl.BlockSpec((d, tf), lambda i, j: (0, j)),
            pl.BlockSpec((tf, d), lambda i, j: (j, 0)),
        ] + cast_specs,
        out_specs=[row] + cast_specs,
        out_shape=[jax.ShapeDtypeStruct((m, d), F32)] + cast_shapes,
        scratch_shapes=[pltpu.VMEM((tm, d), BF16)],
        compiler_params=pltpu.CompilerParams(
            dimension_semantics=("parallel", "arbitrary"),
            vmem_limit_bytes=min(vmem + (8 << 20), V7X_VMEM_BYTES - (2 << 20)),
        ),
        name="ffn",
    )(x, g.reshape(1, d), wg, wu, wd, *[c.src for c in casts])
    return outs[0], list(outs[1:])


def _zoh(logdt, a_re, a_im):
    dt = jnp.exp(logdt)
    lr = jnp.minimum(a_re, -1e-4)
    li = a_im
    mag = jnp.exp(lr * dt)
    ang = li * dt
    abr = mag * jnp.cos(ang)
    abi = mag * jnp.sin(ang)
    den = lr * lr + li * li
    xr = abr - 1.0
    xi = abi
    zr = (xr * lr + xi * li) / den
    zi = (xi * lr - xr * li) / den
    return abr, abi, zr, zi


def _s5_build_params(ldt_ref, are_ref, aim_ref, btre_ref, btim_ref, cre_ref, cim_ref,
                     bblk_ref, cblk_ref, ct_ref, abr_ref, abi_ref):
    ngroup, p, n = btre_ref.shape
    bg = SSM_BLOCK_GROUPS
    abr, abi, zr, zi = _zoh(ldt_ref[...], are_ref[...], aim_ref[...])
    abr_ref[...] = abr.reshape(abr_ref.shape)
    abi_ref[...] = abi.reshape(abi_ref.shape)
    per_row = V7X_LANES // n
    zero = jnp.zeros((p, n), F32)

    def place(piece, gi):
        return jnp.concatenate([piece if s == gi % per_row else zero for s in range(per_row)], axis=1)

    bblk_ref[...] = jnp.zeros_like(bblk_ref)
    for k in range(ngroup // bg):
        ct_ref[...] = jnp.zeros_like(ct_ref)
        for gi in range(bg):
            g = k * bg + gi
            row, lo = g // per_row, (g % per_row) * n
            zr_g = zr[row:row + 1, lo:lo + n]
            zi_g = zi[row:row + 1, lo:lo + n]
            br = btre_ref[g]
            bi = btim_ref[g]
            rows = slice(gi * p, (gi + 1) * p)
            re = slice((gi // per_row) * V7X_LANES, (gi // per_row + 1) * V7X_LANES)
            im = slice(bg * n + re.start, bg * n + re.stop)
            bblk_ref[k, rows, re] = place(zr_g * br - zi_g * bi, gi).astype(BF16)
            bblk_ref[k, rows, im] = place(zr_g * bi + zi_g * br, gi).astype(BF16)
            ct_ref[rows, re] = place(cre_ref[g], gi)
            ct_ref[rows, im] = place(-cim_ref[g], gi)
        cblk_ref[k] = ct_ref[...].T.astype(BF16)


def _s5_kernel(*refs, tc, nblk, bch, bst, chunks_per_seq, n_cast):
    zs_ref, d_ref, wglu_ref, gout_ref = refs[:4]
    raw_params = refs[4:11]
    cast_src, o_ref, cast_dst = refs[11:11 + n_cast], refs[11 + n_cast], refs[12 + n_cast:12 + 2 * n_cast]
    (dre0_ref, dim0_ref, u0_ref, dre1_ref, dim1_ref, u1_ref, st_ref,
     bblk_ref, cblk_ref, ct_ref, abr_ref, abi_ref) = refs[12 + 2 * n_cast:]
    ntile = bst // V7X_LANES
    nb = tc // V7X_SUBLANES
    step_id = pl.program_id(0)
    bufs = ((dre0_ref, dim0_ref, u0_ref), (dre1_ref, dim1_ref, u1_ref))
    _run_casts(cast_src, cast_dst)

    @pl.when(step_id == 0)
    def _():
        _s5_build_params(*raw_params, bblk_ref, cblk_ref, ct_ref, abr_ref, abi_ref)
        st_ref[...] = jnp.zeros_like(st_ref)
        for ref in bufs[1]:
            ref[...] = jnp.zeros_like(ref)

    def step(wr, rd):
        dre_w, dim_w, u_w = wr
        dre_r, dim_r, u_r = rd

        keep = jnp.where((step_id - 1) % chunks_per_seq == 0, 0.0, 1.0).astype(F32)
        ar = [abr_ref[k] for k in range(nblk)]
        ai = [abi_ref[k] for k in range(nblk)]
        s = [st_ref[i] * keep for i in range(2 * nblk)]
        for tb in range(nb):
            for r in range(V7X_SUBLANES):
                rows = slice(r * SCAN_ROW_PITCH, r * SCAN_ROW_PITCH + ntile)
                for k in range(nblk):
                    sre, sim = s[2 * k], s[2 * k + 1]
                    nre = ar[k] * sre - ai[k] * sim + dre_r[k, tb, rows, :]
                    nim = ar[k] * sim + ai[k] * sre + dim_r[k, tb, rows, :]
                    dre_r[k, tb, rows, :] = nre
                    dim_r[k, tb, rows, :] = nim
                    s[2 * k], s[2 * k + 1] = nre, nim
        for i in range(2 * nblk):
            st_ref[i] = s[i]

        u = zs_ref[0]
        u_w[...] = u
        ub = u.astype(BF16)
        for k in range(nblk):
            drv = jnp.dot(ub[:, k * bch:(k + 1) * bch], bblk_ref[k], preferred_element_type=F32)
            for j in range(ntile):
                lo = j * V7X_LANES
                tile = pl.ds(j, V7X_SUBLANES, stride=SCAN_ROW_PITCH)
                dre_w[k, :, tile, :] = drv[:, lo:lo + V7X_LANES].reshape(nb, V7X_SUBLANES, V7X_LANES)
                dim_w[k, :, tile, :] = drv[:, bst + lo:bst + lo + V7X_LANES].reshape(nb, V7X_SUBLANES, V7X_LANES)

        ys = []
        for k in range(nblk):
            parts = []
            for ref in (dre_r, dim_r):
                for j in range(ntile):
                    tile = pl.ds(j, V7X_SUBLANES, stride=SCAN_ROW_PITCH)
                    parts.append(ref[k, :, tile, :].reshape(tc, V7X_LANES))
            lhs = jnp.concatenate(parts, axis=1).astype(BF16)
            ys.append(jnp.dot(lhs, cblk_ref[k], preferred_element_type=F32))
        y = jnp.concatenate(ys, axis=1) + d_ref[...] * u_r[...]
        y = jax.nn.gelu(y)
        y = y * jax.nn.sigmoid(jnp.dot(y.astype(BF16), wglu_ref[...], preferred_element_type=F32))
        o_ref[0] = _rms(y, gout_ref[...]).astype(BF16)

    @pl.when(step_id % 2 == 0)
    def _():
        step(bufs[0], bufs[1])

    @pl.when(step_id % 2 == 1)
    def _():
        step(bufs[1], bufs[0])


def _s5(zs3, log_dt, a_re, a_im, b_re, b_im, c_re, c_im, dskip, wglu, gout, casts=(), *, tc):
    b, l, dssm = zs3.shape
    ngroup, nstate, p = b_re.shape
    nblk, bch, bst = ngroup // SSM_BLOCK_GROUPS, SSM_BLOCK_GROUPS * p, SSM_BLOCK_GROUPS * nstate
    ntile = bst // V7X_LANES
    assert ntile == V7X_SUBLANES and tc % V7X_SUBLANES == 0 and V7X_LANES % nstate == 0
    nb = tc // V7X_SUBLANES
    flat = (ngroup * nstate // V7X_LANES, V7X_LANES)
    raw = [jnp.repeat(log_dt, nstate).reshape(flat), a_re.reshape(flat), a_im.reshape(flat),
           jnp.swapaxes(b_re, 1, 2), jnp.swapaxes(b_im, 1, 2), c_re, c_im]
    nch = l // tc
    last = b * nch - 1
    kern = functools.partial(_s5_kernel, tc=tc, nblk=nblk, bch=bch, bst=bst, chunks_per_seq=nch,
                             n_cast=len(casts))
    cast_specs, cast_shapes = _cast_specs(casts)
    dbuf = pltpu.VMEM((nblk, nb, V7X_SUBLANES * SCAN_ROW_PITCH, V7X_LANES), F32)
    ubuf = pltpu.VMEM((tc, dssm), F32)

    def in_chunk(s):
        c = jnp.minimum(s, last)
        return (c // nch, c % nch, 0)

    def out_chunk(s):
        c = jnp.maximum(s - 1, 0)
        return (c // nch, c % nch, 0)

    outs = pl.pallas_call(
        kern,
        grid=(b * nch + 1,),
        in_specs=[
            pl.BlockSpec((1, tc, dssm), in_chunk),
            _resident((1, dssm)),
            _resident(wglu.shape),
            _resident((1, dssm)),
        ] + [_resident(a.shape) for a in raw] + cast_specs,
        out_specs=[pl.BlockSpec((1, tc, dssm), out_chunk)] + cast_specs,
        out_shape=[jax.ShapeDtypeStruct((b, l, dssm), BF16)] + cast_shapes,
        scratch_shapes=[
            dbuf, dbuf, ubuf, dbuf, dbuf, ubuf,
            pltpu.VMEM((2 * nblk, V7X_SUBLANES, V7X_LANES), F32),
            pltpu.VMEM((nblk, bch, 2 * bst), BF16),
            pltpu.VMEM((nblk, 2 * bst, bch), BF16),
            pltpu.VMEM((bch, 2 * bst), F32),
            pltpu.VMEM((nblk, V7X_SUBLANES, V7X_LANES), F32),
            pltpu.VMEM((nblk, V7X_SUBLANES, V7X_LANES), F32),
        ],
        compiler_params=pltpu.CompilerParams(
            dimension_semantics=("arbitrary",),
            vmem_limit_bytes=56 << 20,
        ),
        name="s5",
    )(zs3, dskip.reshape(1, dssm), wglu, gout.reshape(1, dssm), *raw,
      *[c.src for c in casts])
    return outs[0], list(outs[1:])


def _gmlp_kernel(h_ref, gmix_ref, wssm_ref, wu_ref, wv_ref, gv_ref, ws_ref, bs_ref, gout_ref, o_ref, zs_ref,
                 *, tm, dg):
    hn = _rms(h_ref[...], gmix_ref[...]).astype(BF16)
    zv = jnp.dot(hn, wv_ref[...], preferred_element_type=F32)
    zu = jnp.dot(hn, wu_ref[...], preferred_element_type=F32)
    zs_ref[...] = jnp.dot(hn, wssm_ref[...], preferred_element_type=F32)
    u = jax.nn.gelu(zu)
    v = jax.nn.gelu(zv)
    vc = v - jnp.mean(v, axis=-1, keepdims=True)
    vn = vc * lax.rsqrt(jnp.mean(vc * vc, axis=-1, keepdims=True) + EPS) * gv_ref[...]
    vb = vn.astype(BF16)

    nh = dg // GMLP_CHUNK
    t_idx = lax.broadcasted_iota(jnp.int32, (GMLP_CHUNK, GMLP_CHUNK), 0)
    s_idx = lax.broadcasted_iota(jnp.int32, (GMLP_CHUNK, GMLP_CHUNK), 1)
    causal = t_idx >= s_idx
    nchunk = tm // GMLP_CHUNK
    cols = []
    for hd in range(nh):
        c0 = hd * GMLP_CHUNK
        wm = jnp.where(causal, ws_ref[hd], 0.0).astype(BF16)
        v_h = jnp.concatenate([vb[c * GMLP_CHUNK:(c + 1) * GMLP_CHUNK, c0:c0 + GMLP_CHUNK]
                               for c in range(nchunk)], axis=1)
        s_h = jnp.dot(wm, v_h, preferred_element_type=F32)
        cols.append(jnp.concatenate([s_h[:, c * GMLP_CHUNK:(c + 1) * GMLP_CHUNK] + bs_ref[hd]
                                     for c in range(nchunk)], axis=0))
    yg = u * jnp.concatenate(cols, axis=1)
    o_ref[...] = _rms(yg, gout_ref[...]).astype(BF16)


def _gmlp(h, gmix, win, gv, ws, bs_full, gout, *, tm):
    m, d = h.shape
    dg = gv.shape[0]
    dssm = win.shape[1] - 2 * dg
    ublk = dssm // dg
    kern = functools.partial(_gmlp_kernel, tm=tm, dg=dg)
    return pl.pallas_call(
        kern,
        grid=(m // tm,),
        in_specs=[
            pl.BlockSpec((tm, d), lambda i: (i, 0)),
            _resident((1, d)),
            pl.BlockSpec((d, dssm), lambda i: (0, 0), pipeline_mode=pl.Buffered(1)),
            pl.BlockSpec((d, dg), lambda i: (0, ublk), pipeline_mode=pl.Buffered(1)),
            pl.BlockSpec((d, dg), lambda i: (0, ublk + 1), pipeline_mode=pl.Buffered(1)),
            _resident((1, dg)),
            _resident(ws.shape),
            _resident(bs_full.shape),
            _resident((1, dg)),
        ],
        out_specs=[pl.BlockSpec((tm, dg), lambda i: (i, 0)), pl.BlockSpec((tm, dssm), lambda i: (i, 0))],
        out_shape=[jax.ShapeDtypeStruct((m, dg), BF16), jax.ShapeDtypeStruct((m, dssm), F32)],
        compiler_params=pltpu.CompilerParams(
            dimension_semantics=("parallel",),
            vmem_limit_bytes=56 << 20,
        ),
        name="gmlp",
    )(h, gmix.reshape(1, d), win, win, win, gv.reshape(1, dg), ws, bs_full, gout.reshape(1, dg))


def _outproj_kernel(h_ref, ys_ref, yg_ref, wo_ref, o_ref, *, ds):
    acc = jnp.dot(ys_ref[...], wo_ref[:ds, :], preferred_element_type=F32)
    acc += jnp.dot(yg_ref[...], wo_ref[ds:, :], preferred_element_type=F32)
    o_ref[...] = h_ref[...] + acc


def _outproj(h, ys, yg, wo, *, tm):
    m, d = h.shape
    ds = ys.shape[1]
    dg = yg.shape[1]
    kern = functools.partial(_outproj_kernel, ds=ds)
    return pl.pallas_call(
        kern,
        grid=(m // tm,),
        in_specs=[
            pl.BlockSpec((tm, d), lambda i: (i, 0)),
            pl.BlockSpec((tm, ds), lambda i: (i, 0)),
            pl.BlockSpec((tm, dg), lambda i: (i, 0)),
            _resident(wo.shape),
        ],
        out_specs=pl.BlockSpec((tm, d), lambda i: (i, 0)),
        out_shape=jax.ShapeDtypeStruct((m, d), F32),
        compiler_params=pltpu.CompilerParams(
            dimension_semantics=("parallel",),
            vmem_limit_bytes=48 << 20,
        ),
        name="outproj",
    )(h, ys, yg, wo)


def _ple_kernel(h_ref, p_ref, gple_ref, wg_ref, wp_ref, gfin_ref, o_ref):
    h = h_ref[...]
    hn = _rms(h, gple_ref[...]).astype(BF16)
    gate = jax.nn.sigmoid(jnp.dot(hn, wg_ref[...], preferred_element_type=F32))
    pp = jnp.dot(p_ref[...].astype(BF16), wp_ref[...], preferred_element_type=F32)
    o_ref[...] = _rms(h + gate * pp, gfin_ref[...])


def _ple(h, p, gple, wg, wp, gfin, *, tm):
    m, d = h.shape
    dp = p.shape[1]
    return pl.pallas_call(
        _ple_kernel,
        grid=(m // tm,),
        in_specs=[
            pl.BlockSpec((tm, d), lambda i: (i, 0)),
            pl.BlockSpec((tm, dp), lambda i: (i, 0)),
            _resident((1, d)),
            _resident(wg.shape),
            _resident(wp.shape),
            _resident((1, d)),
        ],
        out_specs=pl.BlockSpec((tm, d), lambda i: (i, 0)),
        out_shape=jax.ShapeDtypeStruct((m, d), F32),
        compiler_params=pltpu.CompilerParams(
            dimension_semantics=("parallel",),
            vmem_limit_bytes=48 << 20,
        ),
        name="ple",
    )(h, p, gple.reshape(1, d), wg, wp, gfin.reshape(1, d))


def kernel(x, p, norm_ffn1, w1_gate, w1_up, w1_down, norm_mix, w_in, ssm_log_dt, ssm_a_re, ssm_a_im, ssm_b_re, ssm_b_im, ssm_c_re, ssm_c_im, ssm_d, ssm_w_glu, gmlp_norm_v, gmlp_w_s, gmlp_b_s, norm_ssm_out, norm_gmlp_out, w_out, norm_ffn2, w2_gate, w2_up, w2_down, norm_ple, w_ple_gate, w_ple_proj, norm_final):
    bsz, seqlen, d = x.shape
    depth = p.shape[0]
    m = bsz * seqlen
    dssm = ssm_d.shape[1]
    assert depth == 1, "the per-layer embedding kernel fuses the final norm"
    h = x.reshape(m, d)
    for i in range(depth):
        nrow, nf = m // FFN_TM, w2_gate.shape[2] // FFN_TF
        jobs = [_slab_job(w, nrow * nf // 2, lambda r, c: (r * nf + c) // 2)
                for w in (w2_gate[i], w2_up[i], w2_down[i], w_in[i], ssm_w_glu[i])]
        h, (w2g, w2u, w2d, w_in_b, wglu_b) = _ffn(
            h, norm_ffn1[i], w1_gate[i].astype(BF16), w1_up[i].astype(BF16), w1_down[i].astype(BF16),
            jobs, tm=FFN_TM, tf=FFN_TF)

        nh, ck = gmlp_b_s.shape[1:]
        bs_full = jnp.broadcast_to(gmlp_b_s[i][:, :, None], (nh, ck, ck))
        yg, zs = _gmlp(h, norm_mix[i], w_in_b, gmlp_norm_v[i], gmlp_w_s[i], bs_full, norm_gmlp_out[i], tm=ROW_TM)

        nsteps = bsz * (seqlen // S5_TC) + 1
        jobs = [_slab_job(w, nsteps) for w in (w_out[i], w_ple_gate[i], w_ple_proj[i])]
        ys, (w_out_b, wpg_b, wpp_b) = _s5(
            zs.reshape(bsz, seqlen, dssm),
            ssm_log_dt[i], ssm_a_re[i], ssm_a_im[i], ssm_b_re[i], ssm_b_im[i], ssm_c_re[i], ssm_c_im[i],
            ssm_d[i], wglu_b, norm_ssm_out[i], jobs, tc=S5_TC)

        h = _outproj(h, ys.reshape(m, dssm), yg, w_out_b, tm=ROW_TM)

        h, _ = _ffn(h, norm_ffn2[i], w2g, w2u, w2d, tm=FFN_TM, tf=FFN_TF)

        h = _ple(h, p[i].reshape(m, -1), norm_ple[i], wpg_b, wpp_b, norm_final, tm=ROW_TM)
    return h.reshape(bsz, seqlen, d)
```
